```python
import math
import jax, jax.numpy as jnp
from jax import lax
import numpy as np

D_MODEL = 1024
BATCH = 1
SEQ = 16384
DEPTH = 1

MIX_DIM = D_MODEL
RWKV_HEADS = 8
RWKV_HEAD_DIM = 64
RWKV_DIM = RWKV_HEADS * RWKV_HEAD_DIM
LORA_W = 64
LORA_A = 64
LORA_G = 128
DIFF_HEADS = 4
DIFF_HEAD_DIM = 64
DIFF_DIM = DIFF_HEADS * 2 * DIFF_HEAD_DIM
RWKV_COLS = 3 * RWKV_DIM + LORA_W + LORA_A + LORA_G
DIFF_COLS = 3 * DIFF_DIM
IN_COLS = RWKV_COLS + DIFF_COLS
Q_BLOCK = 128
D_FF = 2816
CONV_WIDTH = 3
DECAY_SCALE = math.exp(-0.5)
RWKV_LN_EPS = 64e-5
NORM_EPS = 1e-6
SUBLN_EPS = 1e-5
NEG_INF = -1e30

kernel_name = "hybrid_rwkv7_diffattn_convffn"


def rms_norm(x, g, eps=NORM_EPS):
    x32 = x.astype(jnp.float32)
    y = x32 * lax.rsqrt(jnp.mean(x32 * x32, axis=-1, keepdims=True) + eps)
    return (y * g.astype(jnp.float32)).astype(x.dtype)


def alibi_slopes(n):
    return jnp.array([2.0 ** (-8.0 * (i + 1) / n) for i in range(n)], jnp.float32)


def token_shift(p):
    return jnp.pad(p[:, :-1], ((0, 0), (1, 0), (0, 0)))


def rwkv7_mix(p, mu, w_decay_up, w_decay0, w_iclr_up, w_iclr0, w_gate_up,
              k_k, k_a, r_k, ln_x_w, ln_x_b):
    B, S, _ = p.shape
    p = p.astype(jnp.float32)
    xs = p + (token_shift(p) - p) * mu
    cuts = [RWKV_DIM, 2 * RWKV_DIM, 3 * RWKV_DIM, 3 * RWKV_DIM + LORA_W,
            3 * RWKV_DIM + LORA_W + LORA_A]
    r, k, v, w_lo, a_lo, g_lo = jnp.split(xs, cuts, axis=-1)
    decay = jnp.exp(-DECAY_SCALE * jax.nn.sigmoid(w_decay0 + jnp.tanh(w_lo) @ w_decay_up))
    a = jax.nn.sigmoid(w_iclr0 + a_lo @ w_iclr_up)
    g = jax.nn.sigmoid(g_lo) @ w_gate_up
    hs = (B, S, RWKV_HEADS, RWKV_HEAD_DIM)
    kk = (k * k_k).reshape(hs)
    kk = kk / jnp.maximum(jnp.linalg.norm(kk, axis=-1, keepdims=True), 1e-12)
    k = k * (1.0 + (a - 1.0) * k_a)
    r, k, v, decay, a = (t.reshape(hs) for t in (r, k, v, decay, a))

    def step(state, inp):
        r_t, w_t, k_t, v_t, kk_t, a_t = inp
        sa = jnp.einsum('bhvk,bhk->bhv', state, -kk_t)
        state = (state * w_t[:, :, None, :]
                 + sa[..., None] * (kk_t * a_t)[:, :, None, :]
                 + v_t[..., None] * k_t[:, :, None, :])
        return state, jnp.einsum('bhvk,bhk->bhv', state, r_t)

    seq_major = tuple(jnp.moveaxis(t, 1, 0) for t in (r, decay, k, v, kk, a))
    s0 = jnp.zeros((B, RWKV_HEADS, RWKV_HEAD_DIM, RWKV_HEAD_DIM), jnp.float32)
    _, y = lax.scan(step, s0, seq_major)
    y = jnp.moveaxis(y, 0, 1)
    y = y + jnp.sum(r * k * r_k, axis=-1, keepdims=True) * v
    mean = jnp.mean(y, axis=-1, keepdims=True)
    var = jnp.mean(jnp.square(y - mean), axis=-1, keepdims=True)
    y = ((y - mean) * lax.rsqrt(var + RWKV_LN_EPS)).reshape(B, S, RWKV_DIM)
    y = y * ln_x_w + ln_x_b
    return y * g


def diff_attention(q, k, v, lam, slopes):
    B, S, H, _, d = q.shape
    nb = S // Q_BLOCK
    scale = d ** -0.5
    qh = q.transpose(0, 2, 3, 1, 4)
    kh = k.transpose(0, 2, 3, 1, 4).astype(jnp.float32)
    vh = v.transpose(0, 2, 1, 3).astype(jnp.float32)
    qb = qh.reshape(B, H, 2, nb, Q_BLOCK, d).transpose(3, 0, 1, 2, 4, 5)
    kpos = jnp.arange(S)

    def block(args):
        qblk, i = args
        qpos = i * Q_BLOCK + jnp.arange(Q_BLOCK)
        dist = (qpos[:, None] - kpos[None, :]).astype(jnp.float32)
        s = jnp.einsum('bhcqd,bhckd->bhcqk', qblk.astype(jnp.float32), kh) * scale
        s = s - slopes[None, :, None, None, None] * dist
        s = jnp.where(dist >= 0, s, NEG_INF)
        pr = jax.nn.softmax(s, axis=-1)
        attn = pr[:, :, 0] - lam * pr[:, :, 1]
        return jnp.einsum('bhqk,bhkv->bhqv', attn, vh)

    out = lax.map(block, (qb, jnp.arange(nb)))
    return out.transpose(1, 0, 3, 2, 4).reshape(B, S, H, 2 * d)


def causal_dwconv(h, w, b):
    C = h.shape[-1]
    y = lax.conv_general_dilated(h, w[:, None, :].astype(h.dtype), window_strides=(1,),
                                 padding=((CONV_WIDTH - 1, 0),),
                                 dimension_numbers=('NWC', 'WIO', 'NWC'),
                                 feature_group_count=C)
    return y + b.astype(h.dtype)


def setup_inputs(seed: int = 0) -> dict:
    key = jax.random.key(seed)
    ks = jax.random.split(key, 32)
    L = DEPTH

    def nrm(k, shape, s):
        return jax.random.normal(k, shape, jnp.float32) * s

    return {
        "x": nrm(ks[0], (BATCH, SEQ, D_MODEL), 1.0),
        "ln_attn_pre": 1.0 + nrm(ks[1], (L, D_MODEL), 0.05),
        "w_in": nrm(ks[2], (L, D_MODEL, IN_COLS), D_MODEL ** -0.5),
        "mu_shift": jax.random.uniform(ks[3], (L, RWKV_COLS), jnp.float32),
        "w_decay_up": nrm(ks[4], (L, LORA_W, RWKV_DIM), LORA_W ** -0.5),
        "w_decay0": nrm(ks[5], (L, RWKV_DIM), 0.5),
        "w_iclr_up": nrm(ks[6], (L, LORA_A, RWKV_DIM), LORA_A ** -0.5),
        "w_iclr0": nrm(ks[7], (L, RWKV_DIM), 0.1),
        "w_gate_up": nrm(ks[8], (L, LORA_G, RWKV_DIM), LORA_G ** -0.5),
        "k_k": 0.85 + nrm(ks[9], (L, RWKV_DIM), 0.05),
        "k_a": 1.0 + nrm(ks[10], (L, RWKV_DIM), 0.05),
        "r_k": nrm(ks[11], (L, RWKV_HEADS, RWKV_HEAD_DIM), 0.1),
        "ln_x_w": 1.0 + nrm(ks[12], (L, RWKV_DIM), 0.05),
        "ln_x_b": nrm(ks[13], (L, RWKV_DIM), 0.02),
        "lambda_q1": nrm(ks[14], (L, DIFF_HEAD_DIM), 0.1),
        "lambda_k1": nrm(ks[15], (L, DIFF_HEAD_DIM), 0.1),
        "lambda_q2": nrm(ks[16], (L, DIFF_HEAD_DIM), 0.1),
        "lambda_k2": nrm(ks[17], (L, DIFF_HEAD_DIM), 0.1),
        "diff_subln": 1.0 + nrm(ks[18], (L, 2 * DIFF_HEAD_DIM), 0.05),
        "w_out": nrm(ks[19], (L, MIX_DIM, D_MODEL), MIX_DIM ** -0.5),
        "ln_attn_post": 1.0 + nrm(ks[20], (L, D_MODEL), 0.05),
        "ln_ffn_pre": 1.0 + nrm(ks[21], (L, D_MODEL), 0.05),
        "w_up": nrm(ks[22], (L, D_MODEL, 2 * D_FF), D_MODEL ** -0.5),
        "conv_w": nrm(ks[23], (L, CONV_WIDTH, 2 * D_FF), CONV_WIDTH ** -0.5),
        "conv_b": nrm(ks[24], (L, 2 * D_FF), 0.02),
        "w_down": nrm(ks[25], (L, D_FF, D_MODEL), D_FF ** -0.5),
        "ln_ffn_post": 1.0 + nrm(ks[26], (L, D_MODEL), 0.05),
    }


def reference(x, ln_attn_pre, w_in, mu_shift, w_decay_up, w_decay0, w_iclr_up, w_iclr0,
              w_gate_up, k_k, k_a, r_k, ln_x_w, ln_x_b, lambda_q1, lambda_k1, lambda_q2,
              lambda_k2, diff_subln, w_out, ln_attn_post, ln_ffn_pre, w_up, conv_w, conv_b,
              w_down, ln_ffn_post):
    B, S, _ = x.shape
    slopes = alibi_slopes(DIFF_HEADS)
    h = x
    for l in range(DEPTH):
        xn = rms_norm(h, ln_attn_pre[l])
        proj = xn @ w_in[l]
        y_rwkv = rwkv7_mix(proj[..., :RWKV_COLS], mu_shift[l], w_decay_up[l], w_decay0[l],
                           w_iclr_up[l], w_iclr0[l], w_gate_up[l], k_k[l], k_a[l], r_k[l],
                           ln_x_w[l], ln_x_b[l])
        dq, dk, dv = jnp.split(proj[..., RWKV_COLS:], 3, axis=-1)
        dq = dq.reshape(B, S, DIFF_HEADS, 2, DIFF_HEAD_DIM)
        dk = dk.reshape(B, S, DIFF_HEADS, 2, DIFF_HEAD_DIM)
        dv = dv.reshape(B, S, DIFF_HEADS, 2 * DIFF_HEAD_DIM)
        lam_init = 0.8 - 0.6 * math.exp(-0.3 * l)
        lam = (jnp.exp(jnp.sum(lambda_q1[l] * lambda_k1[l]).astype(jnp.float32))
               - jnp.exp(jnp.sum(lambda_q2[l] * lambda_k2[l]).astype(jnp.float32)) + lam_init)
        o = diff_attention(dq, dk, dv, lam, slopes)
        o = rms_norm(o, diff_subln[l], eps=SUBLN_EPS) * (1.0 - lam_init)
        mix = jnp.concatenate([y_rwkv, o.reshape(B, S, DIFF_DIM)], axis=-1).astype(h.dtype)
        h = h + rms_norm(mix @ w_out[l], ln_attn_post[l])
        hn = rms_norm(h, ln_ffn_pre[l])
        u = causal_dwconv(hn @ w_up[l], conv_w[l], conv_b[l])
        gate, val = jnp.split(u, 2, axis=-1)
        f = (jax.nn.gelu(gate, approximate=True) * val) @ w_down[l]
        h = h + rms_norm(f, ln_ffn_post[l])
    return h
```

```python
import functools
import math

import jax
import jax.numpy as jnp
from jax import lax
from jax.experimental import pallas as pl
from jax.experimental.pallas import tpu as pltpu

F32 = jnp.float32
BF16 = jnp.bfloat16

D_MODEL = 1024
RWKV_HEADS = 8
RWKV_HEAD_DIM = 64
RWKV_DIM = RWKV_HEADS * RWKV_HEAD_DIM
LORA_W = 64
LORA_A = 64
LORA_G = 128
DIFF_HEADS = 4
DIFF_HEAD_DIM = 64
DIFF_DIM = DIFF_HEADS * 2 * DIFF_HEAD_DIM
RWKV_COLS = 3 * RWKV_DIM + LORA_W + LORA_A + LORA_G
D_FF = 2816
CONV_WIDTH = 3
DECAY_SCALE = math.exp(-0.5)
RWKV_LN_EPS = 64e-5
NORM_EPS = 1e-6
SUBLN_EPS = 1e-5
NEG_INF = -1e30
LAM_INIT = 0.8 - 0.6 * math.exp(-0.3 * 0)

LANES = 128
VMEM_LIMIT = 56 * 1024 * 1024

ATT_TQ = 512
POS_BLOCK = 1024
POS_SPLIT = 256
N_MAPS = 2 * DIFF_HEADS
QK_PAD = 128
V_PAD = 256

CHUNK = 64
RWKV_ROWS = 128
N_PAIRS = RWKV_HEADS // 2

PROJ_ROWS = 512
FFN_ROWS = 512
FFN_CHUNK = 256


def _dot(a, b):
    return jnp.dot(a, b, preferred_element_type=F32)


def _dot_nt(a, b):
    return lax.dot_general(a, b, (((1,), (1,)), ((), ())), preferred_element_type=F32)


def _dot_tn(a, b):
    return lax.dot_general(a, b, (((0,), (0,)), ((), ())), preferred_element_type=F32)


def _const_spec(shape):
    nd = len(shape)
    return pl.BlockSpec(shape, lambda *_: (0,) * nd)


def _inproj_kernel(x_ref, g_ref, w_ref, qpat_ref, kpat_ref, vpat_ref,
                   rw_ref, q_ref, k_ref, v_ref):
    x = x_ref[...]
    ms = jnp.mean(x * x, axis=-1, keepdims=True)
    xn = (x * lax.rsqrt(ms + NORM_EPS) * g_ref[...]).astype(BF16)
    c0 = RWKV_COLS
    c1 = c0 + N_MAPS * QK_PAD
    c2 = c1 + N_MAPS * QK_PAD
    c3 = c2 + DIFF_HEADS * V_PAD
    rw_ref[...] = _dot(xn, w_ref[:, 0:c0])
    q_ref[...] = (_dot(xn, w_ref[:, c0:c1]) + qpat_ref[...]).astype(BF16)
    k_ref[...] = (_dot(xn, w_ref[:, c1:c2]) + kpat_ref[...]).astype(BF16)
    v_ref[...] = (_dot(xn, w_ref[:, c2:c3]) + vpat_ref[...]).astype(BF16)


def _alibi_slopes():
    return [2.0 ** (-8.0 * (i + 1) / DIFF_HEADS) for i in range(DIFF_HEADS)]


def _alibi_patterns():
    pos = jnp.arange(POS_BLOCK, dtype=jnp.int32)
    lo = (pos % POS_SPLIT).astype(F32)
    hi = (pos - pos % POS_SPLIT).astype(F32)
    one = jnp.ones((POS_BLOCK,), F32)
    qpat = jnp.zeros((POS_BLOCK, N_MAPS, QK_PAD), F32)
    kpat = jnp.zeros((POS_BLOCK, N_MAPS, QK_PAD), F32)
    d = DIFF_HEAD_DIM
    for m in range(N_MAPS):
        slope = _alibi_slopes()[m // 2]
        qpat = qpat.at[:, m, d + 0].set(-slope * lo).at[:, m, d + 1].set(-slope * hi)
        qpat = qpat.at[:, m, d + 2].set(one).at[:, m, d + 3].set(one)
        kpat = kpat.at[:, m, d + 0].set(one).at[:, m, d + 1].set(one)
        kpat = kpat.at[:, m, d + 2].set(slope * lo).at[:, m, d + 3].set(slope * hi)
    vpat = jnp.zeros((1, DIFF_HEADS, V_PAD), F32).at[:, :, 2 * d].set(1.0)
    return (qpat.reshape(POS_BLOCK, N_MAPS * QK_PAD), kpat.reshape(POS_BLOCK, N_MAPS * QK_PAD),
            vpat.reshape(1, DIFF_HEADS * V_PAD))


def _pack_w_in(w_in):
    d = DIFF_HEAD_DIM
    w_rw = w_in[:, :RWKV_COLS]
    wq = w_in[:, RWKV_COLS:RWKV_COLS + DIFF_DIM].reshape(D_MODEL, N_MAPS, d) * (d ** -0.5)
    wk = w_in[:, RWKV_COLS + DIFF_DIM:RWKV_COLS + 2 * DIFF_DIM].reshape(D_MODEL, N_MAPS, d)
    wv = w_in[:, RWKV_COLS + 2 * DIFF_DIM:].reshape(D_MODEL, DIFF_HEADS, 2 * d)
    wq = jnp.pad(wq, ((0, 0), (0, 0), (0, QK_PAD - d))).reshape(D_MODEL, N_MAPS * QK_PAD)
    wk = jnp.pad(wk, ((0, 0), (0, 0), (0, QK_PAD - d))).reshape(D_MODEL, N_MAPS * QK_PAD)
    wv = jnp.pad(wv, ((0, 0), (0, 0), (0, V_PAD - 2 * d))).reshape(D_MODEL, DIFF_HEADS * V_PAD)
    return jnp.concatenate([w_rw, wq, wk, wv], axis=1).astype(BF16)


def _in_projection(x2, g, w_in):
    S = x2.shape[0]
    tm = PROJ_ROWS
    w = _pack_w_in(w_in)
    qpat, kpat, vpat = _alibi_patterns()
    nq = N_MAPS * QK_PAD
    nv = DIFF_HEADS * V_PAD
    rep = POS_BLOCK // tm
    return pl.pallas_call(
        _inproj_kernel,
        grid=(S // tm,),
        in_specs=[
            pl.BlockSpec((tm, D_MODEL), lambda i: (i, 0)),
            _const_spec((1, D_MODEL)),
            pl.BlockSpec(w.shape, lambda i: (0, 0), pipeline_mode=pl.Buffered(1)),
            pl.BlockSpec((tm, nq), lambda i: (i % rep, 0)),
            pl.BlockSpec((tm, nq), lambda i: (i % rep, 0)),
            _const_spec((1, nv)),
        ],
        out_specs=[
            pl.BlockSpec((tm, RWKV_COLS), lambda i: (i, 0)),
            pl.BlockSpec((tm, nq), lambda i: (i, 0)),
            pl.BlockSpec((tm, nq), lambda i: (i, 0)),
            pl.BlockSpec((tm, nv), lambda i: (i, 0)),
        ],
        out_shape=[
            jax.ShapeDtypeStruct((S, RWKV_COLS), F32),
            jax.ShapeDtypeStruct((S, nq), BF16),
            jax.ShapeDtypeStruct((S, nq), BF16),
            jax.ShapeDtypeStruct((S, nv), BF16),
        ],
        compiler_params=pltpu.CompilerParams(
            dimension_semantics=("arbitrary",), vmem_limit_bytes=VMEM_LIMIT),
        name="in_projection",
    )(x2, g.reshape(1, D_MODEL), w, qpat, kpat, vpat)


def _rwkv_kernel(rw_ref, mu_ref, wwa_ref, wd0_ref, wa0_ref, wg_ref, kk_ref, ka_ref, rk_ref,
                 lnw_ref, lnb_ref, hsum_ref, cum_ref, y_ref, prev_sc, state_sc, ych_sc):
    T = RWKV_ROWS
    C = CHUNK
    D = RWKV_DIM

    @pl.when(pl.program_id(0) == 0)
    def _():
        prev_sc[...] = jnp.zeros(prev_sc.shape, F32)
        state_sc[...] = jnp.zeros(state_sc.shape, F32)

    p = rw_ref[...]
    row = lax.broadcasted_iota(jnp.int32, (8, RWKV_COLS), 0)
    shifted = pltpu.roll(p, 1, axis=0)
    first = jnp.where(row == 0, prev_sc[...], shifted[0:8])
    shifted = jnp.concatenate([first, shifted[8:]], axis=0)
    prev_sc[...] = jnp.broadcast_to(p[T - 1:T, :], prev_sc.shape)
    xs = p + (shifted - p) * mu_ref[...]

    r = xs[:, 0:D]
    k = xs[:, D:2 * D]
    v = xs[:, 2 * D:3 * D]
    wa = xs[:, 3 * D:3 * D + LANES]
    g_lo = xs[:, 3 * D + LANES:3 * D + 2 * LANES]

    lane = lax.broadcasted_iota(jnp.int32, (T, LANES), 1)
    wa_act = jnp.where(lane < LORA_W, jnp.tanh(wa), wa).astype(BF16)
    pre = _dot(wa_act, wwa_ref[...])
    lw = -DECAY_SCALE * jax.nn.sigmoid(wd0_ref[...] + pre[:, 0:D])
    a = jax.nn.sigmoid(wa0_ref[...] + pre[:, D:2 * D])
    g = _dot(jax.nn.sigmoid(g_lo).astype(BF16), wg_ref[...])

    hsum = hsum_ref[...]
    kk = k * kk_ref[...]
    n2 = _dot((kk * kk).astype(BF16), hsum)
    kk = kk * jnp.minimum(lax.rsqrt(n2), 1e12)
    k2 = k * (1.0 + (a - 1.0) * ka_ref[...])
    bonus = _dot((r * k2 * rk_ref[...]).astype(BF16), hsum)

    lw_hi = lw.astype(BF16)
    lw_lo = (lw - lw_hi.astype(F32)).astype(BF16)
    cum = _dot(cum_ref[...], lw_hi) + _dot(cum_ref[...], lw_lo)
    L = cum[0:T]
    LC = cum[T:2 * T]
    e_fwd = jnp.exp(L)
    e_prev = jnp.exp(L - lw)
    e_inv = jnp.exp(-L)
    e_end = jnp.exp(LC - L)
    p_end = jnp.exp(LC)
    kka = kk * a
    r_t = (r * e_fwd).astype(BF16)
    al_t = (-kk * e_prev)
    k_t = (k2 * e_inv).astype(BF16)
    be_t = (kka * e_inv).astype(BF16)
    k_e = (k2 * e_end).astype(BF16)
    be_e = (kka * e_end).astype(BF16)
    v_b = v.astype(BF16)

    rr = lax.broadcasted_iota(jnp.int32, (2 * C, LANES), 0)
    ll = lax.broadcasted_iota(jnp.int32, (2 * C, LANES), 1)
    bd_mask = (rr < C) == (ll < C)
    tri_r = lax.broadcasted_iota(jnp.int32, (C, LANES), 0)
    tri_l = lax.broadcasted_iota(jnp.int32, (C, LANES), 1) & (C - 1)
    strict = tri_l < tri_r
    incl = tri_l <= tri_r

    def bd(z):
        zb = z.astype(BF16)
        return jnp.where(bd_mask, jnp.concatenate([zb, zb], axis=0), jnp.zeros((), BF16))

    for c in range(T // C):
        rs = slice(c * C, (c + 1) * C)
        for pr in range(N_PAIRS):
            ls = slice(pr * LANES, (pr + 1) * LANES)
            al_c = al_t[rs, ls]
            xs_c = jnp.concatenate([al_c.astype(BF16), r_t[rs, ls]], axis=0)
            y_bd = jnp.concatenate([bd(be_t[rs, ls]), bd(k_t[rs, ls])], axis=0)
            gram = _dot_nt(xs_c, y_bd)
            a_ab = jnp.where(strict, gram[0:C, 0:LANES], 0.0)
            a_ak = jnp.where(strict, gram[0:C, LANES:2 * LANES], 0.0)
            a_rb = jnp.where(incl, gram[C:2 * C, 0:LANES], 0.0)
            a_rk = jnp.where(incl, gram[C:2 * C, LANES:2 * LANES], 0.0)
            v_bd = bd(v_b[rs, ls])

            x_u = al_c
            x_0 = _dot(a_ak.astype(BF16), v_bd)
            a_pow = a_ab
            n_steps = int(math.log2(C))
            for step in range(n_steps):
                a_pow_b = a_pow.astype(BF16)
                rhs = jnp.concatenate([bd(x_u), bd(x_0)], axis=1)
                upd = _dot(a_pow_b, rhs)
                x_u = x_u + upd[:, 0:LANES]
                x_0 = x_0 + upd[:, LANES:2 * LANES]
                if step + 1 < n_steps:
                    a_pow = _dot(a_pow_b, bd(a_pow))

            st = state_sc[pr]
            st_b = st.astype(BF16)
            u = _dot_nt(x_u.astype(BF16), st_b) + x_0
            y = (_dot_nt(r_t[rs, ls], st_b)
                 + _dot(jnp.concatenate([a_rb, a_rk], axis=1).astype(BF16),
                        jnp.concatenate([bd(u), v_bd], axis=0)))
            ych_sc[rs, ls] = y
            vu = jnp.concatenate([v_b[rs, ls], u.astype(BF16)], axis=0)
            kb = jnp.concatenate([k_e[rs, ls], be_e[rs, ls]], axis=0)
            upd = _dot_tn(vu, kb)
            state_sc[pr] = st * p_end[c * C:c * C + 1, ls] + jnp.where(bd_mask, upd, 0.0)

    y = ych_sc[...] + bonus * v
    inv_n = 1.0 / RWKV_HEAD_DIM
    mean = _dot(y.astype(BF16), hsum) * inv_n
    yc = y - mean
    var = _dot((yc * yc).astype(BF16), hsum) * inv_n
    yn = yc * lax.rsqrt(var + RWKV_LN_EPS)
    y_ref[...] = ((yn * lnw_ref[...] + lnb_ref[...]) * g).astype(y_ref.dtype)


def _rwkv_mix(rw, mu, w_decay_up, w_decay0, w_iclr_up, w_iclr0, w_gate_up, k_k, k_a, r_k,
              ln_x_w, ln_x_b):
    S = rw.shape[0]
    T = RWKV_ROWS
    D = RWKV_DIM
    wwa = jnp.zeros((LANES, 2 * D), F32)
    wwa = wwa.at[0:LORA_W, 0:D].set(w_decay_up).at[LORA_W:LORA_W + LORA_A, D:2 * D].set(w_iclr_up)
    hid = jnp.arange(D) // RWKV_HEAD_DIM
    hsum = (hid[:, None] == hid[None, :]).astype(BF16)
    t = jnp.arange(T)
    same = (t[:, None] // CHUNK) == (t[None, :] // CHUNK)
    cum = jnp.concatenate([same & (t[None, :] <= t[:, None]), same], axis=0).astype(BF16)
    row = lambda z: z.reshape(1, -1).astype(F32)
    args = (rw, row(mu), wwa.astype(BF16), row(w_decay0), row(w_iclr0), w_gate_up.astype(BF16),
            row(k_k), row(k_a), row(r_k), row(ln_x_w), row(ln_x_b), hsum, cum)
    in_specs = [pl.BlockSpec((T, RWKV_COLS), lambda i: (i, 0))]
    in_specs += [_const_spec(z.shape) for z in args[1:]]
    return pl.pallas_call(
        _rwkv_kernel,
        grid=(S // T,),
        in_specs=in_specs,
        out_specs=pl.BlockSpec((T, D), lambda i: (i, 0)),
        out_shape=jax.ShapeDtypeStruct((S, D), BF16),
        scratch_shapes=[
            pltpu.VMEM((8, RWKV_COLS), F32),
            pltpu.VMEM((N_PAIRS, LANES, LANES), F32),
            pltpu.VMEM((T, D), F32),
        ],
        compiler_params=pltpu.CompilerParams(
            dimension_semantics=("arbitrary",), vmem_limit_bytes=VMEM_LIMIT),
        name="rwkv7_mix",
    )(*args)


def _attn_kernel(slopes_ref, q_ref, k_ref, v_ref, lq1_ref, lk1_ref, lq2_ref, lk2_ref, sub_ref,
                 o_ref, m_sc, acc_sc):
    tq = ATT_TQ
    h = pl.program_id(0)
    i = pl.program_id(1)
    slope = slopes_ref[h]
    q0 = i * tq
    q_base = (q0 // POS_BLOCK) * POS_BLOCK

    m_sc[...] = jnp.full(m_sc.shape, NEG_INF, F32)
    acc_sc[...] = jnp.zeros(acc_sc.shape, F32)
    qs = [q_ref[:, c * QK_PAD:(c + 1) * QK_PAD] for c in range(2)]

    def kv_step(j, masked):
        k0 = pl.multiple_of(j * tq, tq)
        k_base = (k0 // POS_BLOCK) * POS_BLOCK
        shift = slope * (q_base - k_base).astype(F32)
        ks = k_ref[pl.ds(k0, tq), :]
        vs = v_ref[pl.ds(k0, tq), :]
        for c in range(2):
            s = _dot_nt(qs[c], ks[:, c * QK_PAD:(c + 1) * QK_PAD])
            if masked:
                rr = lax.broadcasted_iota(jnp.int32, (tq, tq), 0)
                cc = lax.broadcasted_iota(jnp.int32, (tq, tq), 1)
                s = jnp.where(rr >= cc, s, NEG_INF)
            m_old = m_sc[c]
            m_new = jnp.maximum(m_old, jnp.max(s, axis=-1, keepdims=True) - shift)
            pexp = jnp.exp(s - (m_new + shift))
            alpha = jnp.exp(m_old - m_new)
            acc_sc[c] = alpha * acc_sc[c] + _dot(pexp.astype(BF16), vs)
            m_sc[c] = m_new

    def body(j, carry):
        kv_step(j, False)
        return carry

    lax.fori_loop(0, i, body, 0)
    kv_step(i, True)

    d2 = 2 * DIFF_HEAD_DIM
    lam = (jnp.exp(jnp.sum(lq1_ref[...] * lk1_ref[...], axis=-1, keepdims=True))
           - jnp.exp(jnp.sum(lq2_ref[...] * lk2_ref[...], axis=-1, keepdims=True)) + LAM_INIT)
    a0 = acc_sc[0]
    a1 = acc_sc[1]
    o = a0[:, 0:d2] / a0[:, d2:d2 + 1] - lam * (a1[:, 0:d2] / a1[:, d2:d2 + 1])
    ms = jnp.mean(o * o, axis=-1, keepdims=True)
    o = o * lax.rsqrt(ms + SUBLN_EPS) * sub_ref[...] * (1.0 - LAM_INIT)
    o_ref[...] = o.astype(o_ref.dtype)


def _diff_attention(qp, kp, vp, lq1, lk1, lq2, lk2, subln):
    S = qp.shape[0]
    tq = ATT_TQ
    d2 = 2 * DIFF_HEAD_DIM
    slopes = jnp.array(_alibi_slopes(), F32)
    vec = lambda z: z.reshape(1, -1).astype(F32)
    resident = functools.partial(pl.BlockSpec, pipeline_mode=pl.Buffered(1))
    return pl.pallas_call(
        _attn_kernel,
        grid=(DIFF_HEADS, S // tq),
        in_specs=[
            pl.BlockSpec(memory_space=pltpu.SMEM),
            pl.BlockSpec((tq, 2 * QK_PAD), lambda h, i: (i, h)),
            resident((S, 2 * QK_PAD), lambda h, i: (0, h)),
            resident((S, V_PAD), lambda h, i: (0, h)),
            _const_spec((1, DIFF_HEAD_DIM)), _const_spec((1, DIFF_HEAD_DIM)),
            _const_spec((1, DIFF_HEAD_DIM)), _const_spec((1, DIFF_HEAD_DIM)),
            _const_spec((1, d2)),
        ],
        out_specs=pl.BlockSpec((tq, d2), lambda h, i: (i, h)),
        out_shape=jax.ShapeDtypeStruct((S, DIFF_DIM), BF16),
        scratch_shapes=[
            pltpu.VMEM((2, tq, 1), F32),
            pltpu.VMEM((2, tq, V_PAD), F32),
        ],
        compiler_params=pltpu.CompilerParams(
            dimension_semantics=("arbitrary", "arbitrary"), vmem_limit_bytes=VMEM_LIMIT),
        name="diff_attention",
    )(slopes, qp, kp, vp, vec(lq1), vec(lk1), vec(lq2), vec(lk2), vec(subln))


def _gelu_tanh(x):
    return 0.5 * x * (1.0 + jnp.tanh(math.sqrt(2.0 / math.pi) * (x + 0.044715 * (x * x * x))))


def _ffn_kernel(x_ref, yr_ref, yo_ref, wo_ref, gpost_ref, gpre_ref, wup_ref, cw_ref, cb_ref,
                wdn_ref, gffn_ref, out_ref, tail_sc, hn_sc, acc_sc):
    tm = FFN_ROWS
    cw = FFN_CHUNK

    @pl.when(pl.program_id(0) == 0)
    def _():
        tail_sc[...] = jnp.zeros(tail_sc.shape, F32)

    def rms(z, g_row):
        ms = jnp.mean(z * z, axis=-1, keepdims=True)
        return z * lax.rsqrt(ms + NORM_EPS) * g_row

    half = RWKV_DIM
    mixed = _dot(yr_ref[...], wo_ref[0:half, :]) + _dot(yo_ref[...], wo_ref[half:2 * half, :])
    h = x_ref[...] + rms(mixed, gpost_ref[...])
    hn_sc[...] = rms(h, gpre_ref[...]).astype(BF16)
    acc_sc[...] = jnp.zeros(acc_sc.shape, F32)

    row8 = lax.broadcasted_iota(jnp.int32, (8, cw), 0)

    def conv(col0):
        u = _dot(hn_sc[...], wup_ref[:, col0:col0 + cw])
        tail = tail_sc[:, col0:col0 + cw]
        tail_sc[:, col0:col0 + cw] = u[tm - 8:tm]
        u1 = pltpu.roll(u, 1, axis=0)
        u2 = pltpu.roll(u, 2, axis=0)
        f1 = jnp.where(row8 < 1, pltpu.roll(tail, 1, axis=0), u1[0:8])
        f2 = jnp.where(row8 < 2, pltpu.roll(tail, 2, axis=0), u2[0:8])
        u1 = jnp.concatenate([f1, u1[8:]], axis=0)
        u2 = jnp.concatenate([f2, u2[8:]], axis=0)
        w = cw_ref[:, col0:col0 + cw]
        return (u2 * w[0:1] + u1 * w[1:2] + u * w[2:3]) + cb_ref[:, col0:col0 + cw]

    for c in range(D_FF // cw):
        gate = conv(c * cw)
        val = conv(D_FF + c * cw)
        act = (_gelu_tanh(gate) * val).astype(BF16)
        acc_sc[...] += _dot(act, wdn_ref[c * cw:(c + 1) * cw, :])

    out_ref[...] = h + rms(acc_sc[...], gffn_ref[...])


def _out_and_ffn(x2, y_rwkv, y_attn, w_out, g_post, g_pre, w_up, conv_w, conv_b, w_down, g_ffn):
    S = x2.shape[0]
    tm = FFN_ROWS
    row = lambda z: z.reshape(1, -1).astype(F32)
    single = functools.partial(pl.BlockSpec, pipeline_mode=pl.Buffered(1))
    const1 = lambda shape: single(shape, lambda i: (0,) * len(shape))
    return pl.pallas_call(
        _ffn_kernel,
        grid=(S // tm,),
        in_specs=[
            pl.BlockSpec((tm, D_MODEL), lambda i: (i, 0)),
            pl.BlockSpec((tm, RWKV_DIM), lambda i: (i, 0)),
            pl.BlockSpec((tm, DIFF_DIM), lambda i: (i, 0)),
            const1((D_MODEL, D_MODEL)),
            _const_spec((1, D_MODEL)), _const_spec((1, D_MODEL)),
            const1((D_MODEL, 2 * D_FF)),
            _const_spec((CONV_WIDTH, 2 * D_FF)), _const_spec((1, 2 * D_FF)),
            const1((D_FF, D_MODEL)),
            _const_spec((1, D_MODEL)),
        ],
        out_specs=pl.BlockSpec((tm, D_MODEL), lambda i: (i, 0)),
        out_shape=jax.ShapeDtypeStruct((S, D_MODEL), F32),
        scratch_shapes=[
            pltpu.VMEM((8, 2 * D_FF), F32),
            pltpu.VMEM((tm, D_MODEL), BF16),
            pltpu.VMEM((tm, D_MODEL), F32),
        ],
        compiler_params=pltpu.CompilerParams(
            dimension_semantics=("arbitrary",), vmem_limit_bytes=VMEM_LIMIT),
        name="out_ffn",
    )(x2, y_rwkv, y_attn, w_out.astype(BF16), row(g_post), row(g_pre), w_up.astype(BF16),
      conv_w.astype(F32), row(conv_b), w_down.astype(BF16), row(g_ffn))


def kernel(x, ln_attn_pre, w_in, mu_shift, w_decay_up, w_decay0, w_iclr_up, w_iclr0, w_gate_up,
           k_k, k_a, r_k, ln_x_w, ln_x_b, lambda_q1, lambda_k1, lambda_q2, lambda_k2, diff_subln,
           w_out, ln_attn_post, ln_ffn_pre, w_up, conv_w, conv_b, w_down, ln_ffn_post):
    B, S, _ = x.shape
    assert B == 1 and w_in.shape[0] == 1, "single batch, depth 1"
    assert S % POS_BLOCK == 0
    x2 = x[0]
    rw, qp, kp, vp = _in_projection(x2, ln_attn_pre[0], w_in[0])
    y_rwkv = _rwkv_mix(rw, mu_shift[0], w_decay_up[0], w_decay0[0], w_iclr_up[0], w_iclr0[0],
                       w_gate_up[0], k_k[0], k_a[0], r_k[0], ln_x_w[0], ln_x_b[0])
    y_attn = _diff_attention(qp, kp, vp, lambda_q1[0], lambda_k1[0], lambda_q2[0], lambda_k2[0],
                             diff_subln[0])
    out = _out_and_ffn(x2, y_rwkv, y_attn, w_out[0], ln_attn_post[0], ln_ffn_pre[0], w_up[0],
                       conv_w[0], conv_b[0], w_down[0], ln_ffn_post[0])
    return out[None]
```

```python
import functools
import math

import jax
import jax.numpy as jnp
import numpy as np
from jax import lax
from jax.experimental import pallas as pl
from jax.experimental.pallas import tpu as pltpu

F32 = jnp.float32
BF16 = jnp.bfloat16

D_MODEL = 1024
RWKV_HEADS = 8
RWKV_HEAD_DIM = 64
RWKV_DIM = RWKV_HEADS * RWKV_HEAD_DIM
LORA_W = 64
LORA_A = 64
LORA_G = 128
DIFF_HEADS = 4
DIFF_HEAD_DIM = 64
DIFF_DIM = DIFF_HEADS * 2 * DIFF_HEAD_DIM
RWKV_COLS = 3 * RWKV_DIM + LORA_W + LORA_A + LORA_G
D_FF = 2816
CONV_WIDTH = 3
DECAY_SCALE = math.exp(-0.5)
RWKV_LN_EPS = 64e-5
NORM_EPS = 1e-6
SUBLN_EPS = 1e-5
NEG_INF = -1e30
LAM_INIT = 0.8 - 0.6 * math.exp(-0.3 * 0)

LANES = 128
VMEM_LIMIT = 56 * 1024 * 1024

ATT_TQ = 512
POS_BLOCK = 1024
POS_SPLIT = 256
N_MAPS = 2 * DIFF_HEADS
QK_PAD = 128
V_PAD = 256

CHUNK = 64
RWKV_ROWS = 128
N_PAIRS = RWKV_HEADS // 2

PROJ_ROWS = 512
FFN_ROWS = 512
FFN_CHUNK = 256


def _dot(a, b):
    return jnp.dot(a, b, preferred_element_type=F32)


def _dot_nt(a, b):
    return lax.dot_general(a, b, (((1,), (1,)), ((), ())), preferred_element_type=F32)


def _dot_tn(a, b):
    return lax.dot_general(a, b, (((0,), (0,)), ((), ())), preferred_element_type=F32)


def _const_spec(shape):
    nd = len(shape)
    return pl.BlockSpec(shape, lambda *_: (0,) * nd)


def _inproj_kernel(x_ref, g_ref, w_ref, qpat_ref, kpat_ref, vpat_ref,
                   rw_ref, q_ref, k_ref, v_ref):
    x = x_ref[...]
    ms = jnp.mean(x * x, axis=-1, keepdims=True)
    xn = (x * lax.rsqrt(ms + NORM_EPS) * g_ref[...]).astype(BF16)
    c0 = RWKV_COLS
    c1 = c0 + N_MAPS * QK_PAD
    c2 = c1 + N_MAPS * QK_PAD
    c3 = c2 + DIFF_HEADS * V_PAD
    rw_ref[...] = _dot(xn, w_ref[:, 0:c0])
    q_ref[...] = (_dot(xn, w_ref[:, c0:c1]) + qpat_ref[...]).astype(BF16)
    k_ref[...] = (_dot(xn, w_ref[:, c1:c2]) + kpat_ref[...]).astype(BF16)
    v_ref[...] = (_dot(xn, w_ref[:, c2:c3]) + vpat_ref[...]).astype(BF16)


def _alibi_slopes():
    return [2.0 ** (-8.0 * (i + 1) / DIFF_HEADS) for i in range(DIFF_HEADS)]


def _alibi_patterns():
    pos = np.arange(POS_BLOCK)
    lo = (pos % POS_SPLIT).astype(np.float32)
    hi = (pos - pos % POS_SPLIT).astype(np.float32)
    qpat = np.zeros((POS_BLOCK, N_MAPS, QK_PAD), np.float32)
    kpat = np.zeros((POS_BLOCK, N_MAPS, QK_PAD), np.float32)
    d = DIFF_HEAD_DIM
    for m in range(N_MAPS):
        slope = _alibi_slopes()[m // 2]
        qpat[:, m, d + 0] = -slope * lo
        qpat[:, m, d + 1] = -slope * hi
        qpat[:, m, d + 2:d + 4] = 1.0
        kpat[:, m, d + 0:d + 2] = 1.0
        kpat[:, m, d + 2] = slope * lo
        kpat[:, m, d + 3] = slope * hi
    vpat = np.zeros((1, DIFF_HEADS, V_PAD), np.float32)
    vpat[:, :, 2 * d] = 1.0
    return (jnp.asarray(qpat.reshape(POS_BLOCK, N_MAPS * QK_PAD), BF16),
            jnp.asarray(kpat.reshape(POS_BLOCK, N_MAPS * QK_PAD), BF16),
            jnp.asarray(vpat.reshape(1, DIFF_HEADS * V_PAD)))


def _pack_w_in(w_in):
    d = DIFF_HEAD_DIM
    w_rw = w_in[:, :RWKV_COLS]
    wq = w_in[:, RWKV_COLS:RWKV_COLS + DIFF_DIM].reshape(D_MODEL, N_MAPS, d) * (d ** -0.5)
    wk = w_in[:, RWKV_COLS + DIFF_DIM:RWKV_COLS + 2 * DIFF_DIM].reshape(D_MODEL, N_MAPS, d)
    wv = w_in[:, RWKV_COLS + 2 * DIFF_DIM:].reshape(D_MODEL, DIFF_HEADS, 2 * d)
    wq = jnp.pad(wq, ((0, 0), (0, 0), (0, QK_PAD - d))).reshape(D_MODEL, N_MAPS * QK_PAD)
    wk = jnp.pad(wk, ((0, 0), (0, 0), (0, QK_PAD - d))).reshape(D_MODEL, N_MAPS * QK_PAD)
    wv = jnp.pad(wv, ((0, 0), (0, 0), (0, V_PAD - 2 * d))).reshape(D_MODEL, DIFF_HEADS * V_PAD)
    return jnp.concatenate([w_rw, wq, wk, wv], axis=1).astype(BF16)


def _in_projection(x2, g, w_in):
    S = x2.shape[0]
    tm = PROJ_ROWS
    w = _pack_w_in(w_in)
    qpat, kpat, vpat = _alibi_patterns()
    nq = N_MAPS * QK_PAD
    nv = DIFF_HEADS * V_PAD
    rep = POS_BLOCK // tm
    return pl.pallas_call(
        _inproj_kernel,
        grid=(S // tm,),
        in_specs=[
            pl.BlockSpec((tm, D_MODEL), lambda i: (i, 0)),
            _const_spec((1, D_MODEL)),
            pl.BlockSpec(w.shape, lambda i: (0, 0), pipeline_mode=pl.Buffered(1)),
            pl.BlockSpec((tm, nq), lambda i: (i % rep, 0)),
            pl.BlockSpec((tm, nq), lambda i: (i % rep, 0)),
            _const_spec((1, nv)),
        ],
        out_specs=[
            pl.BlockSpec((tm, RWKV_COLS), lambda i: (i, 0)),
            pl.BlockSpec((tm, nq), lambda i: (i, 0)),
            pl.BlockSpec((tm, nq), lambda i: (i, 0)),
            pl.BlockSpec((tm, nv), lambda i: (i, 0)),
        ],
        out_shape=[
            jax.ShapeDtypeStruct((S, RWKV_COLS), F32),
            jax.ShapeDtypeStruct((S, nq), BF16),
            jax.ShapeDtypeStruct((S, nq), BF16),
            jax.ShapeDtypeStruct((S, nv), BF16),
        ],
        compiler_params=pltpu.CompilerParams(
            dimension_semantics=("arbitrary",), vmem_limit_bytes=VMEM_LIMIT),
        name="in_projection",
    )(x2, g.reshape(1, D_MODEL), w, qpat, kpat, vpat)


def _rwkv_kernel(rw_ref, mu_ref, wwa_ref, wd0_ref, wa0_ref, wg_ref, kk_ref, ka_ref, rk_ref,
                 lnw_ref, lnb_ref, hsum_ref, cum_ref, y_ref, prev_sc, state_sc, ych_sc):
    T = RWKV_ROWS
    C = CHUNK
    D = RWKV_DIM

    @pl.when(pl.program_id(0) == 0)
    def _():
        prev_sc[...] = jnp.zeros(prev_sc.shape, F32)
        state_sc[...] = jnp.zeros(state_sc.shape, F32)

    p = rw_ref[...]
    row = lax.broadcasted_iota(jnp.int32, (8, RWKV_COLS), 0)
    shifted = pltpu.roll(p, 1, axis=0)
    first = jnp.where(row == 0, prev_sc[...], shifted[0:8])
    shifted = jnp.concatenate([first, shifted[8:]], axis=0)
    prev_sc[...] = jnp.broadcast_to(p[T - 1:T, :], prev_sc.shape)
    xs = p + (shifted - p) * mu_ref[...]

    r = xs[:, 0:D]
    k = xs[:, D:2 * D]
    v = xs[:, 2 * D:3 * D]
    wa = xs[:, 3 * D:3 * D + LANES]
    g_lo = xs[:, 3 * D + LANES:3 * D + 2 * LANES]

    lane = lax.broadcasted_iota(jnp.int32, (T, LANES), 1)
    wa_act = jnp.where(lane < LORA_W, jnp.tanh(wa), wa).astype(BF16)
    pre = _dot(wa_act, wwa_ref[...])
    lw = -DECAY_SCALE * jax.nn.sigmoid(wd0_ref[...] + pre[:, 0:D])
    a = jax.nn.sigmoid(wa0_ref[...] + pre[:, D:2 * D])
    g = _dot(jax.nn.sigmoid(g_lo).astype(BF16), wg_ref[...])

    hsum = hsum_ref[...]
    kk = k * kk_ref[...]
    n2 = _dot((kk * kk).astype(BF16), hsum)
    kk = kk * jnp.minimum(lax.rsqrt(n2), 1e12)
    k2 = k * (1.0 + (a - 1.0) * ka_ref[...])
    bonus = _dot((r * k2 * rk_ref[...]).astype(BF16), hsum)

    lw_hi = lw.astype(BF16)
    lw_lo = (lw - lw_hi.astype(F32)).astype(BF16)
    cum = _dot(cum_ref[...], lw_hi) + _dot(cum_ref[...], lw_lo)
    L = cum[0:T]
    LC = cum[T:2 * T]
    e_fwd = jnp.exp(L)
    e_prev = jnp.exp(L - lw)
    e_inv = jnp.exp(-L)
    e_end = jnp.exp(LC - L)
    p_end = jnp.exp(LC)
    kka = kk * a
    r_t = (r * e_fwd).astype(BF16)
    al_t = (-kk * e_prev)
    k_t = (k2 * e_inv).astype(BF16)
    be_t = (kka * e_inv).astype(BF16)
    k_e = (k2 * e_end).astype(BF16)
    be_e = (kka * e_end).astype(BF16)
    v_b = v.astype(BF16)

    rr = lax.broadcasted_iota(jnp.int32, (2 * C, LANES), 0)
    ll = lax.broadcasted_iota(jnp.int32, (2 * C, LANES), 1)
    bd_mask = (rr < C) == (ll < C)
    tri_r = lax.broadcasted_iota(jnp.int32, (C, LANES), 0)
    tri_l = lax.broadcasted_iota(jnp.int32, (C, LANES), 1) & (C - 1)
    strict = tri_l < tri_r
    incl = tri_l <= tri_r

    def bd(z):
        zb = z.astype(BF16)
        return jnp.where(bd_mask, jnp.concatenate([zb, zb], axis=0), jnp.zeros((), BF16))

    for c in range(T // C):
        rs = slice(c * C, (c + 1) * C)
        for pr in range(N_PAIRS):
            ls = slice(pr * LANES, (pr + 1) * LANES)
            al_c = al_t[rs, ls]
            xs_c = jnp.concatenate([al_c.astype(BF16), r_t[rs, ls]], axis=0)
            y_bd = jnp.concatenate([bd(be_t[rs, ls]), bd(k_t[rs, ls])], axis=0)
            gram = _dot_nt(xs_c, y_bd)
            a_ab = jnp.where(strict, gram[0:C, 0:LANES], 0.0)
            a_ak = jnp.where(strict, gram[0:C, LANES:2 * LANES], 0.0)
            a_rb = jnp.where(incl, gram[C:2 * C, 0:LANES], 0.0)
            a_rk = jnp.where(incl, gram[C:2 * C, LANES:2 * LANES], 0.0)
            v_bd = bd(v_b[rs, ls])

            x_u = al_c
            x_0 = _dot(a_ak.astype(BF16), v_bd)
            a_pow = a_ab
            n_steps = int(math.log2(C))
            for step in range(n_steps):
                a_pow_b = a_pow.astype(BF16)
                rhs = jnp.concatenate([bd(x_u), bd(x_0)], axis=1)
                upd = _dot(a_pow_b, rhs)
                x_u = x_u + upd[:, 0:LANES]
                x_0 = x_0 + upd[:, LANES:2 * LANES]
                if step + 1 < n_steps:
                    a_pow = _dot(a_pow_b, bd(a_pow))

            st = state_sc[pr]
            st_b = st.astype(BF16)
            u = _dot_nt(x_u.astype(BF16), st_b) + x_0
            y = (_dot_nt(r_t[rs, ls], st_b)
                 + _dot(jnp.concatenate([a_rb, a_rk], axis=1).astype(BF16),
                        jnp.concatenate([bd(u), v_bd], axis=0)))
            ych_sc[rs, ls] = y
            vu = jnp.concatenate([v_b[rs, ls], u.astype(BF16)], axis=0)
            kb = jnp.concatenate([k_e[rs, ls], be_e[rs, ls]], axis=0)
            upd = _dot_tn(vu, kb)
            state_sc[pr] = st * p_end[c * C:c * C + 1, ls] + jnp.where(bd_mask, upd, 0.0)

    y = ych_sc[...] + bonus * v
    inv_n = 1.0 / RWKV_HEAD_DIM
    mean = _dot(y.astype(BF16), hsum) * inv_n
    yc = y - mean
    var = _dot((yc * yc).astype(BF16), hsum) * inv_n
    yn = yc * lax.rsqrt(var + RWKV_LN_EPS)
    y_ref[...] = ((yn * lnw_ref[...] + lnb_ref[...]) * g).astype(y_ref.dtype)


def _rwkv_mix(rw, mu, w_decay_up, w_decay0, w_iclr_up, w_iclr0, w_gate_up, k_k, k_a, r_k,
              ln_x_w, ln_x_b):
    S = rw.shape[0]
    T = RWKV_ROWS
    D = RWKV_DIM
    wwa = jnp.zeros((LANES, 2 * D), F32)
    wwa = wwa.at[0:LORA_W, 0:D].set(w_decay_up).at[LORA_W:LORA_W + LORA_A, D:2 * D].set(w_iclr_up)
    hid = jnp.arange(D) // RWKV_HEAD_DIM
    hsum = (hid[:, None] == hid[None, :]).astype(BF16)
    t = jnp.arange(T)
    same = (t[:, None] // CHUNK) == (t[None, :] // CHUNK)
    cum = jnp.concatenate([same & (t[None, :] <= t[:, None]), same], axis=0).astype(BF16)
    row = lambda z: z.reshape(1, -1).astype(F32)
    args = (rw, row(mu), wwa.astype(BF16), row(w_decay0), row(w_iclr0), w_gate_up.astype(BF16),
            row(k_k), row(k_a), row(r_k), row(ln_x_w), row(ln_x_b), hsum, cum)
    in_specs = [pl.BlockSpec((T, RWKV_COLS), lambda i: (i, 0))]
    in_specs += [_const_spec(z.shape) for z in args[1:]]
    return pl.pallas_call(
        _rwkv_kernel,
        grid=(S // T,),
        in_specs=in_specs,
        out_specs=pl.BlockSpec((T, D), lambda i: (i, 0)),
        out_shape=jax.ShapeDtypeStruct((S, D), BF16),
        scratch_shapes=[
            pltpu.VMEM((8, RWKV_COLS), F32),
            pltpu.VMEM((N_PAIRS, LANES, LANES), F32),
            pltpu.VMEM((T, D), F32),
        ],
        compiler_params=pltpu.CompilerParams(
            dimension_semantics=("arbitrary",), vmem_limit_bytes=VMEM_LIMIT),
        name="rwkv7_mix",
    )(*args)


def _attn_kernel(slopes_ref, q_ref, k_ref, v_ref, lq1_ref, lk1_ref, lq2_ref, lk2_ref, sub_ref,
                 o_ref, m_sc, acc_sc):
    tq = ATT_TQ
    h = pl.program_id(0)
    i = pl.program_id(1)
    slope = slopes_ref[h]
    q0 = i * tq
    q_base = (q0 // POS_BLOCK) * POS_BLOCK

    m_sc[...] = jnp.full(m_sc.shape, NEG_INF, F32)
    acc_sc[...] = jnp.zeros(acc_sc.shape, F32)
    qs = [q_ref[:, c * QK_PAD:(c + 1) * QK_PAD] for c in range(2)]

    def kv_step(j, masked):
        k0 = pl.multiple_of(j * tq, tq)
        k_base = (k0 // POS_BLOCK) * POS_BLOCK
        shift = slope * (q_base - k_base).astype(F32)
        ks = k_ref[pl.ds(k0, tq), :]
        vs = v_ref[pl.ds(k0, tq), :]
        for c in range(2):
            s = _dot_nt(qs[c], ks[:, c * QK_PAD:(c + 1) * QK_PAD])
            if masked:
                rr = lax.broadcasted_iota(jnp.int32, (tq, tq), 0)
                cc = lax.broadcasted_iota(jnp.int32, (tq, tq), 1)
                s = jnp.where(rr >= cc, s, NEG_INF)
            m_old = m_sc[c]
            m_new = jnp.maximum(m_old, jnp.max(s, axis=-1, keepdims=True) - shift)
            pexp = jnp.exp(s - (m_new + shift))
            alpha = jnp.exp(m_old - m_new)
            acc_sc[c] = alpha * acc_sc[c] + _dot(pexp.astype(BF16), vs)
            m_sc[c] = m_new

    def body(j, carry):
        kv_step(j, False)
        return carry

    lax.fori_loop(0, i, body, 0)
    kv_step(i, True)

    d2 = 2 * DIFF_HEAD_DIM
    lam = (jnp.exp(jnp.sum(lq1_ref[...] * lk1_ref[...], axis=-1, keepdims=True))
           - jnp.exp(jnp.sum(lq2_ref[...] * lk2_ref[...], axis=-1, keepdims=True)) + LAM_INIT)
    a0 = acc_sc[0]
    a1 = acc_sc[1]
    o = a0[:, 0:d2] / a0[:, d2:d2 + 1] - lam * (a1[:, 0:d2] / a1[:, d2:d2 + 1])
    ms = jnp.mean(o * o, axis=-1, keepdims=True)
    o = o * lax.rsqrt(ms + SUBLN_EPS) * sub_ref[...] * (1.0 - LAM_INIT)
    o_ref[...] = o.astype(o_ref.dtype)


def _diff_attention(qp, kp, vp, lq1, lk1, lq2, lk2, subln):
    S = qp.shape[0]
    tq = ATT_TQ
    d2 = 2 * DIFF_HEAD_DIM
    slopes = jnp.array(_alibi_slopes(), F32)
    vec = lambda z: z.reshape(1, -1).astype(F32)
    resident = functools.partial(pl.BlockSpec, pipeline_mode=pl.Buffered(1))
    return pl.pallas_call(
        _attn_kernel,
        grid=(DIFF_HEADS, S // tq),
        in_specs=[
            pl.BlockSpec(memory_space=pltpu.SMEM),
            pl.BlockSpec((tq, 2 * QK_PAD), lambda h, i: (i, h)),
            resident((S, 2 * QK_PAD), lambda h, i: (0, h)),
            resident((S, V_PAD), lambda h, i: (0, h)),
            _const_spec((1, DIFF_HEAD_DIM)), _const_spec((1, DIFF_HEAD_DIM)),
            _const_spec((1, DIFF_HEAD_DIM)), _const_spec((1, DIFF_HEAD_DIM)),
            _const_spec((1, d2)),
        ],
        out_specs=pl.BlockSpec((tq, d2), lambda h, i: (i, h)),
        out_shape=jax.ShapeDtypeStruct((S, DIFF_DIM), BF16),
        scratch_shapes=[
            pltpu.VMEM((2, tq, 1), F32),
            pltpu.VMEM((2, tq, V_PAD), F32),
        ],
        compiler_params=pltpu.CompilerParams(
            dimension_semantics=("arbitrary", "arbitrary"), vmem_limit_bytes=VMEM_LIMIT),
        name="diff_attention",
    )(slopes, qp, kp, vp, vec(lq1), vec(lk1), vec(lq2), vec(lk2), vec(subln))


def _gelu_tanh(x):
    return 0.5 * x * (1.0 + jnp.tanh(math.sqrt(2.0 / math.pi) * (x + 0.044715 * (x * x * x))))


def _ffn_kernel(x_ref, yr_ref, yo_ref, wo_ref, gpost_ref, gpre_ref, wup_ref, cw_ref, cb_ref,
                wdn_ref, gffn_ref, out_ref, tail_sc, hn_sc, acc_sc):
    tm = FFN_ROWS
    cw = FFN_CHUNK

    @pl.when(pl.program_id(0) == 0)
    def _():
        tail_sc[...] = jnp.zeros(tail_sc.shape, F32)

    def rms(z, g_row):
        ms = jnp.mean(z * z, axis=-1, keepdims=True)
        return z * lax.rsqrt(ms + NORM_EPS) * g_row

    half = RWKV_DIM
    mixed = _dot(yr_ref[...], wo_ref[0:half, :]) + _dot(yo_ref[...], wo_ref[half:2 * half, :])
    h = x_ref[...] + rms(mixed, gpost_ref[...])
    hn_sc[...] = rms(h, gpre_ref[...]).astype(BF16)
    acc_sc[...] = jnp.zeros(acc_sc.shape, F32)

    row8 = lax.broadcasted_iota(jnp.int32, (8, cw), 0)

    def conv(col0):
        u = _dot(hn_sc[...], wup_ref[:, col0:col0 + cw])
        tail = tail_sc[:, col0:col0 + cw]
        tail_sc[:, col0:col0 + cw] = u[tm - 8:tm]
        u1 = pltpu.roll(u, 1, axis=0)
        u2 = pltpu.roll(u, 2, axis=0)
        f1 = jnp.where(row8 < 1, pltpu.roll(tail, 1, axis=0), u1[0:8])
        f2 = jnp.where(row8 < 2, pltpu.roll(tail, 2, axis=0), u2[0:8])
        u1 = jnp.concatenate([f1, u1[8:]], axis=0)
        u2 = jnp.concatenate([f2, u2[8:]], axis=0)
        w = cw_ref[:, col0:col0 + cw]
        return (u2 * w[0:1] + u1 * w[1:2] + u * w[2:3]) + cb_ref[:, col0:col0 + cw]

    for c in range(D_FF // cw):
        gate = conv(c * cw)
        val = conv(D_FF + c * cw)
        act = (_gelu_tanh(gate) * val).astype(BF16)
        acc_sc[...] += _dot(act, wdn_ref[c * cw:(c + 1) * cw, :])

    out_ref[...] = h + rms(acc_sc[...], gffn_ref[...])


def _out_and_ffn(x2, y_rwkv, y_attn, w_out, g_post, g_pre, w_up, conv_w, conv_b, w_down, g_ffn):
    S = x2.shape[0]
    tm = FFN_ROWS
    row = lambda z: z.reshape(1, -1).astype(F32)
    single = functools.partial(pl.BlockSpec, pipeline_mode=pl.Buffered(1))
    const1 = lambda shape: single(shape, lambda i: (0,) * len(shape))
    return pl.pallas_call(
        _ffn_kernel,
        grid=(S // tm,),
        in_specs=[
            pl.BlockSpec((tm, D_MODEL), lambda i: (i, 0)),
            pl.BlockSpec((tm, RWKV_DIM), lambda i: (i, 0)),
            pl.BlockSpec((tm, DIFF_DIM), lambda i: (i, 0)),
            const1((D_MODEL, D_MODEL)),
            _const_spec((1, D_MODEL)), _const_spec((1, D_MODEL)),
            const1((D_MODEL, 2 * D_FF)),
            _const_spec((CONV_WIDTH, 2 * D_FF)), _const_spec((1, 2 * D_FF)),
            const1((D_FF, D_MODEL)),
            _const_spec((1, D_MODEL)),
        ],
        out_specs=pl.BlockSpec((tm, D_MODEL), lambda i: (i, 0)),
        out_shape=jax.ShapeDtypeStruct((S, D_MODEL), F32),
        scratch_shapes=[
            pltpu.VMEM((8, 2 * D_FF), F32),
            pltpu.VMEM((tm, D_MODEL), BF16),
            pltpu.VMEM((tm, D_MODEL), F32),
        ],
        compiler_params=pltpu.CompilerParams(
            dimension_semantics=("arbitrary",), vmem_limit_bytes=VMEM_LIMIT),
        name="out_ffn",
    )(x2, y_rwkv, y_attn, w_out.astype(BF16), row(g_post), row(g_pre), w_up.astype(BF16),
      conv_w.astype(F32), row(conv_b), w_down.astype(BF16), row(g_ffn))


def kernel(x, ln_attn_pre, w_in, mu_shift, w_decay_up, w_decay0, w_iclr_up, w_iclr0, w_gate_up,
           k_k, k_a, r_k, ln_x_w, ln_x_b, lambda_q1, lambda_k1, lambda_q2, lambda_k2, diff_subln,
           w_out, ln_attn_post, ln_ffn_pre, w_up, conv_w, conv_b, w_down, ln_ffn_post):
    B, S, _ = x.shape
    assert B == 1 and w_in.shape[0] == 1, "single batch, depth 1"
    assert S % POS_BLOCK == 0
    x2 = x[0]
    rw, qp, kp, vp = _in_projection(x2, ln_attn_pre[0], w_in[0])
    y_rwkv = _rwkv_mix(rw, mu_shift[0], w_decay_up[0], w_decay0[0], w_iclr_up[0], w_iclr0[0],
                       w_gate_up[0], k_k[0], k_a[0], r_k[0], ln_x_w[0], ln_x_b[0])
    y_attn = _diff_attention(qp, kp, vp, lambda_q1[0], lambda_k1[0], lambda_q2[0], lambda_k2[0],
                             diff_subln[0])
    out = _out_and_ffn(x2, y_rwkv, y_attn, w_out[0], ln_attn_post[0], ln_ffn_pre[0], w_up[0],
                       conv_w[0], conv_b[0], w_down[0], ln_ffn_post[0])
    return out[None]
```

```python
import functools
import math

import jax
import jax.numpy as jnp
import numpy as np
from jax import lax
from jax.experimental import pallas as pl
from jax.experimental.pallas import tpu as pltpu

F32 = jnp.float32
BF16 = jnp.bfloat16

D_MODEL = 1024
RWKV_HEADS = 8
RWKV_HEAD_DIM = 64
RWKV_DIM = RWKV_HEADS * RWKV_HEAD_DIM
LORA_W = 64
LORA_A = 64
LORA_G = 128
DIFF_HEADS = 4
DIFF_HEAD_DIM = 64
DIFF_DIM = DIFF_HEADS * 2 * DIFF_HEAD_DIM
RWKV_COLS = 3 * RWKV_DIM + LORA_W + LORA_A + LORA_G
D_FF = 2816
CONV_WIDTH = 3
DECAY_SCALE = math.exp(-0.5)
RWKV_LN_EPS = 64e-5
NORM_EPS = 1e-6
SUBLN_EPS = 1e-5
NEG_INF = -1e30
LAM_INIT = 0.8 - 0.6 * math.exp(-0.3 * 0)

LANES = 128
VMEM_LIMIT = 56 * 1024 * 1024

ATT_TQ = 1024
ATT_TK = 512
ATT_QB = 256
ATT_LOOKAHEAD = 3
POS_BLOCK = 1024
POS_SPLIT = 256
N_MAPS = 2 * DIFF_HEADS
QK_PAD = 128
VT_ROWS = 144

CHUNK = 64
RWKV_ROWS = 128
N_PAIRS = RWKV_HEADS // 2

PROJ_ROWS = 512
FFN_ROWS = 512
FFN_CHUNK = 256


def _dot(a, b):
    return jnp.dot(a, b, preferred_element_type=F32)


def _dot_nt(a, b):
    return lax.dot_general(a, b, (((1,), (1,)), ((), ())), preferred_element_type=F32)


def _dot_tn(a, b):
    return lax.dot_general(a, b, (((0,), (0,)), ((), ())), preferred_element_type=F32)


def _const_spec(shape):
    nd = len(shape)
    return pl.BlockSpec(shape, lambda *_: (0,) * nd)


def _inproj_kernel(x_ref, g_ref, w_ref, wvt_ref, qpat_ref, kpat_ref, vtpat_ref,
                   rw_ref, q_ref, k_ref, vt_ref):
    x = x_ref[...]
    ms = jnp.mean(x * x, axis=-1, keepdims=True)
    xn = (x * lax.rsqrt(ms + NORM_EPS) * g_ref[...]).astype(BF16)
    c0 = RWKV_COLS
    c1 = c0 + N_MAPS * QK_PAD
    c2 = c1 + N_MAPS * QK_PAD
    rw_ref[...] = _dot(xn, w_ref[:, 0:c0])
    q_ref[...] = (_dot(xn, w_ref[:, c0:c1]) + qpat_ref[...]).astype(BF16)
    k_ref[...] = (_dot(xn, w_ref[:, c1:c2]) + kpat_ref[...]).astype(BF16)
    vt = (_dot_nt(wvt_ref[...], xn) + vtpat_ref[...]).astype(BF16)
    for h in range(DIFF_HEADS):
        vt_ref[h, 0] = vt[h * VT_ROWS:(h + 1) * VT_ROWS]


def _alibi_slopes():
    return [2.0 ** (-8.0 * (i + 1) / DIFF_HEADS) for i in range(DIFF_HEADS)]


def _alibi_patterns():
    pos = np.arange(POS_BLOCK)
    lo = (pos % POS_SPLIT).astype(np.float32)
    hi = (pos - pos % POS_SPLIT).astype(np.float32)
    qpat = np.zeros((POS_BLOCK, N_MAPS, QK_PAD), np.float32)
    kpat = np.zeros((POS_BLOCK, N_MAPS, QK_PAD), np.float32)
    d = DIFF_HEAD_DIM
    for m in range(N_MAPS):
        slope = _alibi_slopes()[m // 2]
        qpat[:, m, d + 0] = -slope * lo
        qpat[:, m, d + 1] = -slope * hi
        qpat[:, m, d + 2:d + 4] = 1.0
        kpat[:, m, d + 0:d + 2] = 1.0
        kpat[:, m, d + 2] = slope * lo
        kpat[:, m, d + 3] = slope * hi
    vtpat = np.zeros((DIFF_HEADS, VT_ROWS, PROJ_ROWS), np.float32)
    vtpat[:, 2 * d, :] = 1.0
    return (jnp.asarray(qpat.reshape(POS_BLOCK, N_MAPS * QK_PAD), BF16),
            jnp.asarray(kpat.reshape(POS_BLOCK, N_MAPS * QK_PAD), BF16),
            jnp.asarray(vtpat.reshape(DIFF_HEADS * VT_ROWS, PROJ_ROWS), BF16))


def _pack_w_in(w_in):
    d = DIFF_HEAD_DIM
    w_rw = w_in[:, :RWKV_COLS]
    wq = w_in[:, RWKV_COLS:RWKV_COLS + DIFF_DIM].reshape(D_MODEL, N_MAPS, d) * (d ** -0.5)
    wk = w_in[:, RWKV_COLS + DIFF_DIM:RWKV_COLS + 2 * DIFF_DIM].reshape(D_MODEL, N_MAPS, d)
    wv = w_in[:, RWKV_COLS + 2 * DIFF_DIM:].reshape(D_MODEL, DIFF_HEADS, 2 * d)
    wq = jnp.pad(wq, ((0, 0), (0, 0), (0, QK_PAD - d))).reshape(D_MODEL, N_MAPS * QK_PAD)
    wk = jnp.pad(wk, ((0, 0), (0, 0), (0, QK_PAD - d))).reshape(D_MODEL, N_MAPS * QK_PAD)
    wvt = jnp.pad(wv.transpose(1, 2, 0), ((0, 0), (0, VT_ROWS - 2 * d), (0, 0)))
    return (jnp.concatenate([w_rw, wq, wk], axis=1).astype(BF16),
            wvt.reshape(DIFF_HEADS * VT_ROWS, D_MODEL).astype(BF16))


def _in_projection(x2, g, w_in):
    S = x2.shape[0]
    tm = PROJ_ROWS
    assert tm == ATT_TK, "the transposed value blocks are consumed one per attention kv step"
    w, wvt = _pack_w_in(w_in)
    qpat, kpat, vtpat = _alibi_patterns()
    nq = N_MAPS * QK_PAD
    rep = POS_BLOCK // tm
    single = functools.partial(pl.BlockSpec, pipeline_mode=pl.Buffered(1))
    return pl.pallas_call(
        _inproj_kernel,
        grid=(S // tm,),
        in_specs=[
            pl.BlockSpec((tm, D_MODEL), lambda i: (i, 0)),
            _const_spec((1, D_MODEL)),
            single(w.shape, lambda i: (0, 0)),
            single(wvt.shape, lambda i: (0, 0)),
            pl.BlockSpec((tm, nq), lambda i: (i % rep, 0)),
            pl.BlockSpec((tm, nq), lambda i: (i % rep, 0)),
            _const_spec(vtpat.shape),
        ],
        out_specs=[
            pl.BlockSpec((tm, RWKV_COLS), lambda i: (i, 0)),
            pl.BlockSpec((tm, nq), lambda i: (i, 0)),
            pl.BlockSpec((tm, nq), lambda i: (i, 0)),
            pl.BlockSpec((DIFF_HEADS, 1, VT_ROWS, tm), lambda i: (0, i, 0, 0)),
        ],
        out_shape=[
            jax.ShapeDtypeStruct((S, RWKV_COLS), F32),
            jax.ShapeDtypeStruct((S, nq), BF16),
            jax.ShapeDtypeStruct((S, nq), BF16),
            jax.ShapeDtypeStruct((DIFF_HEADS, S // tm, VT_ROWS, tm), BF16),
        ],
        compiler_params=pltpu.CompilerParams(
            dimension_semantics=("arbitrary",), vmem_limit_bytes=VMEM_LIMIT),
        name="in_projection",
    )(x2, g.reshape(1, D_MODEL), w, wvt, qpat, kpat, vtpat)


def _rwkv_kernel(rw_ref, mu_ref, wwa_ref, wd0_ref, wa0_ref, wg_ref, kk_ref, ka_ref, rk_ref,
                 lnw_ref, lnb_ref, hsum_ref, cum_ref, y_ref, prev_sc, state_sc, ych_sc):
    T = RWKV_ROWS
    C = CHUNK
    D = RWKV_DIM

    @pl.when(pl.program_id(0) == 0)
    def _():
        prev_sc[...] = jnp.zeros(prev_sc.shape, F32)
        state_sc[...] = jnp.zeros(state_sc.shape, F32)

    p = rw_ref[...]
    row = lax.broadcasted_iota(jnp.int32, (8, RWKV_COLS), 0)
    shifted = pltpu.roll(p, 1, axis=0)
    first = jnp.where(row == 0, prev_sc[...], shifted[0:8])
    shifted = jnp.concatenate([first, shifted[8:]], axis=0)
    prev_sc[...] = jnp.broadcast_to(p[T - 1:T, :], prev_sc.shape)
    xs = p + (shifted - p) * mu_ref[...]

    r = xs[:, 0:D]
    k = xs[:, D:2 * D]
    v = xs[:, 2 * D:3 * D]
    wa = xs[:, 3 * D:3 * D + LANES]
    g_lo = xs[:, 3 * D + LANES:3 * D + 2 * LANES]

    lane = lax.broadcasted_iota(jnp.int32, (T, LANES), 1)
    wa_act = jnp.where(lane < LORA_W, jnp.tanh(wa), wa).astype(BF16)
    pre = _dot(wa_act, wwa_ref[...])
    lw = -DECAY_SCALE * jax.nn.sigmoid(wd0_ref[...] + pre[:, 0:D])
    a = jax.nn.sigmoid(wa0_ref[...] + pre[:, D:2 * D])
    g = _dot(jax.nn.sigmoid(g_lo).astype(BF16), wg_ref[...])

    hsum = hsum_ref[...]
    kk = k * kk_ref[...]
    n2 = _dot((kk * kk).astype(BF16), hsum)
    kk = kk * jnp.minimum(lax.rsqrt(n2), 1e12)
    k2 = k * (1.0 + (a - 1.0) * ka_ref[...])
    bonus = _dot((r * k2 * rk_ref[...]).astype(BF16), hsum)

    lw_hi = lw.astype(BF16)
    lw_lo = (lw - lw_hi.astype(F32)).astype(BF16)
    cum = _dot(cum_ref[...], lw_hi) + _dot(cum_ref[...], lw_lo)
    L = cum[0:T]
    LC = cum[T:2 * T]
    e_fwd = jnp.exp(L)
    e_prev = jnp.exp(L - lw)
    e_inv = jnp.exp(-L)
    e_end = jnp.exp(LC - L)
    p_end = jnp.exp(LC)
    kka = kk * a
    r_t = (r * e_fwd).astype(BF16)
    al_t = (-kk * e_prev)
    k_t = (k2 * e_inv).astype(BF16)
    be_t = (kka * e_inv).astype(BF16)
    k_e = (k2 * e_end).astype(BF16)
    be_e = (kka * e_end).astype(BF16)
    v_b = v.astype(BF16)

    rr = lax.broadcasted_iota(jnp.int32, (2 * C, LANES), 0)
    ll = lax.broadcasted_iota(jnp.int32, (2 * C, LANES), 1)
    bd_mask = (rr < C) == (ll < C)
    tri_r = lax.broadcasted_iota(jnp.int32, (C, LANES), 0)
    tri_l = lax.broadcasted_iota(jnp.int32, (C, LANES), 1) & (C - 1)
    strict = tri_l < tri_r
    incl = tri_l <= tri_r

    def bd(z):
        zb = z.astype(BF16)
        return jnp.where(bd_mask, jnp.concatenate([zb, zb], axis=0), jnp.zeros((), BF16))

    n_chunks = T // C
    pairs = range(N_PAIRS)
    blocks = [(c, pr) for c in range(n_chunks) for pr in pairs]
    blk = {(c, pr): (slice(c * C, (c + 1) * C), slice(pr * LANES, (pr + 1) * LANES))
           for c, pr in blocks}

    gram = {}
    for key in blocks:
        sl = blk[key]
        xs_c = jnp.concatenate([al_t[sl].astype(BF16), r_t[sl]], axis=0)
        y_bd = jnp.concatenate([bd(be_t[sl]), bd(k_t[sl])], axis=0)
        gram[key] = _dot_nt(xs_c, y_bd)
    a_pow = {key: jnp.where(strict, gram[key][0:C, 0:LANES], 0.0) for key in blocks}
    a_r = {key: jnp.concatenate([jnp.where(incl, gram[key][C:2 * C, 0:LANES], 0.0),
                                 jnp.where(incl, gram[key][C:2 * C, LANES:2 * LANES], 0.0)],
                                axis=1).astype(BF16) for key in blocks}
    v_bd = {key: bd(v_b[blk[key]]) for key in blocks}

    x_u = {key: al_t[blk[key]] for key in blocks}
    x_0 = {key: _dot(jnp.where(strict, gram[key][0:C, LANES:2 * LANES], 0.0).astype(BF16), v_bd[key])
           for key in blocks}
    n_steps = int(math.log2(C))
    for step in range(n_steps):
        a_pow_b = {key: a_pow[key].astype(BF16) for key in blocks}
        for key in blocks:
            rhs = jnp.concatenate([bd(x_u[key]), bd(x_0[key])], axis=1)
            upd = _dot(a_pow_b[key], rhs)
            x_u[key] = x_u[key] + upd[:, 0:LANES]
            x_0[key] = x_0[key] + upd[:, LANES:2 * LANES]
        if step + 1 < n_steps:
            a_pow = {key: _dot(a_pow_b[key], bd(a_pow[key])) for key in blocks}

    st = {pr: state_sc[pr] for pr in pairs}
    for c in range(n_chunks):
        st_b = {pr: st[pr].astype(BF16) for pr in pairs}
        u = {pr: _dot_nt(x_u[c, pr].astype(BF16), st_b[pr]) + x_0[c, pr] for pr in pairs}
        for pr in pairs:
            sl = blk[c, pr]
            vu = jnp.concatenate([v_b[sl], u[pr].astype(BF16)], axis=0)
            kb = jnp.concatenate([k_e[sl], be_e[sl]], axis=0)
            upd = _dot_tn(vu, kb)
            st[pr] = st[pr] * p_end[c * C:c * C + 1, sl[1]] + jnp.where(bd_mask, upd, 0.0)
        y_s = {pr: _dot_nt(r_t[blk[c, pr]], st_b[pr]) for pr in pairs}
        for pr in pairs:
            ych_sc[blk[c, pr]] = y_s[pr] + _dot(
                a_r[c, pr], jnp.concatenate([bd(u[pr]), v_bd[c, pr]], axis=0))
    for pr in pairs:
        state_sc[pr] = st[pr]

    y = ych_sc[...] + bonus * v
    inv_n = 1.0 / RWKV_HEAD_DIM
    mean = _dot(y.astype(BF16), hsum) * inv_n
    yc = y - mean
    var = _dot((yc * yc).astype(BF16), hsum) * inv_n
    yn = yc * lax.rsqrt(var + RWKV_LN_EPS)
    y_ref[...] = ((yn * lnw_ref[...] + lnb_ref[...]) * g).astype(y_ref.dtype)


def _rwkv_mix(rw, mu, w_decay_up, w_decay0, w_iclr_up, w_iclr0, w_gate_up, k_k, k_a, r_k,
              ln_x_w, ln_x_b):
    S = rw.shape[0]
    T = RWKV_ROWS
    D = RWKV_DIM
    wwa = jnp.zeros((LANES, 2 * D), F32)
    wwa = wwa.at[0:LORA_W, 0:D].set(w_decay_up).at[LORA_W:LORA_W + LORA_A, D:2 * D].set(w_iclr_up)
    hid = jnp.arange(D) // RWKV_HEAD_DIM
    hsum = (hid[:, None] == hid[None, :]).astype(BF16)
    t = jnp.arange(T)
    same = (t[:, None] // CHUNK) == (t[None, :] // CHUNK)
    cum = jnp.concatenate([same & (t[None, :] <= t[:, None]), same], axis=0).astype(BF16)
    row = lambda z: z.reshape(1, -1).astype(F32)
    args = (rw, row(mu), wwa.astype(BF16), row(w_decay0), row(w_iclr0), w_gate_up.astype(BF16),
            row(k_k), row(k_a), row(r_k), row(ln_x_w), row(ln_x_b), hsum, cum)
    in_specs = [pl.BlockSpec((T, RWKV_COLS), lambda i: (i, 0))]
    in_specs += [_const_spec(z.shape) for z in args[1:]]
    return pl.pallas_call(
        _rwkv_kernel,
        grid=(S // T,),
        in_specs=in_specs,
        out_specs=pl.BlockSpec((T, D), lambda i: (i, 0)),
        out_shape=jax.ShapeDtypeStruct((S, D), BF16),
        scratch_shapes=[
            pltpu.VMEM((8, RWKV_COLS), F32),
            pltpu.VMEM((N_PAIRS, LANES, LANES), F32),
            pltpu.VMEM((T, D), F32),
        ],
        compiler_params=pltpu.CompilerParams(
            dimension_semantics=("arbitrary",), vmem_limit_bytes=VMEM_LIMIT),
        name="rwkv7_mix",
    )(*args)


def _attn_kernel(slopes_ref, q_ref, k_ref, vt_ref, lq1_ref, lk1_ref, lq2_ref, lk2_ref, sub_ref,
                 o_ref, acc_sc, s_sc):
    tq = ATT_TQ
    tk = ATT_TK
    qb = ATT_QB
    nb = tq // qb
    kv_per_q = tq // tk
    h = pl.program_id(0)
    i = pl.program_id(1)
    slope = slopes_ref[h]
    q_base = ((i * tq) // POS_BLOCK) * POS_BLOCK

    acc_sc[...] = jnp.zeros(acc_sc.shape, F32)
    chains = [(c, b) for c in range(2) for b in range(nb)]
    qs = [q_ref[b * qb:(b + 1) * qb, c * QK_PAD:(c + 1) * QK_PAD] for c, b in chains]
    n = len(chains)
    ahead = ATT_LOOKAHEAD

    def scores(j, ci, rows):
        c, _ = chains[ci]
        k0 = pl.multiple_of(j * tk, tk)
        kc = k_ref[pl.ds(k0, rows), c * QK_PAD:(c + 1) * QK_PAD]
        return _dot_nt(kc, qs[ci])

    def accumulate(j, ci, s, m_old, kv_offset):
        c, b = chains[ci]
        rows = s.shape[0]
        k_base = ((j * tk) // POS_BLOCK) * POS_BLOCK
        shift = slope * (q_base - k_base).astype(F32)
        if kv_offset is not None:
            kv_i = lax.broadcasted_iota(jnp.int32, (rows, qb), 0) + kv_offset
            q_i = lax.broadcasted_iota(jnp.int32, (rows, qb), 1) + b * qb
            s = jnp.where(kv_i <= q_i, s, NEG_INF)
        m_new = jnp.maximum(m_old, jnp.max(s, axis=0, keepdims=True) - shift)
        p = jnp.exp(s - (m_new + shift)).astype(BF16)
        alpha = jnp.exp(m_old - m_new)
        cols = slice(b * qb, (b + 1) * qb)
        acc_sc[c, :, cols] = alpha * acc_sc[c, :, cols] + _dot(vt_ref[j, :, 0:rows], p)
        return m_new

    def run(tasks, ms, next_tile):
        ms = list(ms)
        early = {}
        for t, (j, ci, rows, kv_offset) in enumerate(tasks):
            nxt = t + ahead
            if nxt < len(tasks):
                early[nxt] = scores(tasks[nxt][0], tasks[nxt][1], tasks[nxt][2])
            elif next_tile is not None:
                s_sc[nxt - len(tasks)] = scores(next_tile, nxt - len(tasks), tk)
            s = s_sc[t, 0:rows] if t < ahead else early.pop(t)
            ms[ci] = accumulate(j, ci, s, ms[ci], kv_offset)
        return tuple(ms)

    for t in range(ahead):
        s_sc[t] = scores(0, t, tk)
    ms0 = tuple(jnp.full((1, qb), NEG_INF, F32) for _ in chains)
    first_diag = i * kv_per_q
    ms = lax.fori_loop(
        0, first_diag,
        lambda j, ms: run([(j, ci, tk, None) for ci in range(n)], ms, j + 1), ms0)

    tasks = []
    for d in range(kv_per_q):
        for ci, (c, b) in enumerate(chains):
            rows = min(tk, (b + 1) * qb - d * tk)
            if rows > 0:
                crosses = d * tk + rows - 1 > b * qb
                tasks.append((first_diag + d, ci, rows, d * tk if crosses else None))
    run(tasks, ms, None)

    d2 = 2 * DIFF_HEAD_DIM
    lam = (jnp.exp(jnp.sum(lq1_ref[...] * lk1_ref[...], axis=-1, keepdims=True))
           - jnp.exp(jnp.sum(lq2_ref[...] * lk2_ref[...], axis=-1, keepdims=True)) + LAM_INIT)
    a0 = acc_sc[0]
    a1 = acc_sc[1]
    o = a0[0:d2] / a0[d2:d2 + 1] - lam * (a1[0:d2] / a1[d2:d2 + 1])
    ms_o = jnp.mean(o * o, axis=0, keepdims=True)
    o = o * lax.rsqrt(ms_o + SUBLN_EPS) * sub_ref[...] * (1.0 - LAM_INIT)
    o_ref[...] = o.T.astype(o_ref.dtype)


def _diff_attention(qp, kp, vt, lq1, lk1, lq2, lk2, subln):
    S = qp.shape[0]
    tq = ATT_TQ
    d2 = 2 * DIFF_HEAD_DIM
    slopes = jnp.array(_alibi_slopes(), F32)
    vec = lambda z: z.reshape(1, -1).astype(F32)
    resident = functools.partial(pl.BlockSpec, pipeline_mode=pl.Buffered(1))
    return pl.pallas_call(
        _attn_kernel,
        grid=(DIFF_HEADS, S // tq),
        in_specs=[
            pl.BlockSpec(memory_space=pltpu.SMEM),
            pl.BlockSpec((tq, 2 * QK_PAD), lambda h, i: (i, h)),
            resident((S, 2 * QK_PAD), lambda h, i: (0, h)),
            resident((None, S // ATT_TK, VT_ROWS, ATT_TK), lambda h, i: (h, 0, 0, 0)),
            _const_spec((1, DIFF_HEAD_DIM)), _const_spec((1, DIFF_HEAD_DIM)),
            _const_spec((1, DIFF_HEAD_DIM)), _const_spec((1, DIFF_HEAD_DIM)),
            _const_spec((d2, 1)),
        ],
        out_specs=pl.BlockSpec((tq, d2), lambda h, i: (i, h)),
        out_shape=jax.ShapeDtypeStruct((S, DIFF_DIM), BF16),
        scratch_shapes=[pltpu.VMEM((2, VT_ROWS, tq), F32),
                        pltpu.VMEM((ATT_LOOKAHEAD, ATT_TK, ATT_QB), F32)],
        compiler_params=pltpu.CompilerParams(
            dimension_semantics=("arbitrary", "arbitrary"), vmem_limit_bytes=VMEM_LIMIT),
        name="diff_attention",
    )(slopes, qp, kp, vt, vec(lq1), vec(lk1), vec(lq2), vec(lk2),
      subln.reshape(d2, 1).astype(F32))


def _gelu_tanh(x):
    return 0.5 * x * (1.0 + jnp.tanh(math.sqrt(2.0 / math.pi) * (x + 0.044715 * (x * x * x))))


def _ffn_kernel(x_ref, yr_ref, yo_ref, wo_ref, gpost_ref, gpre_ref, wup_ref, cw_ref, cb_ref,
                wdn_ref, gffn_ref, out_ref, tail_sc, hn_sc, acc_sc):
    tm = FFN_ROWS
    cw = FFN_CHUNK

    @pl.when(pl.program_id(0) == 0)
    def _():
        tail_sc[...] = jnp.zeros(tail_sc.shape, F32)

    def rms(z, g_row):
        ms = jnp.mean(z * z, axis=-1, keepdims=True)
        return z * lax.rsqrt(ms + NORM_EPS) * g_row

    half = RWKV_DIM
    mixed = _dot(yr_ref[...], wo_ref[0:half, :]) + _dot(yo_ref[...], wo_ref[half:2 * half, :])
    h = x_ref[...] + rms(mixed, gpost_ref[...])
    hn_sc[...] = rms(h, gpre_ref[...]).astype(BF16)
    acc_sc[...] = jnp.zeros(acc_sc.shape, F32)

    row8 = lax.broadcasted_iota(jnp.int32, (8, cw), 0)

    def conv(col0):
        u = _dot(hn_sc[...], wup_ref[:, col0:col0 + cw])
        tail = tail_sc[:, col0:col0 + cw]
        tail_sc[:, col0:col0 + cw] = u[tm - 8:tm]
        u1 = pltpu.roll(u, 1, axis=0)
        u2 = pltpu.roll(u, 2, axis=0)
        f1 = jnp.where(row8 < 1, pltpu.roll(tail, 1, axis=0), u1[0:8])
        f2 = jnp.where(row8 < 2, pltpu.roll(tail, 2, axis=0), u2[0:8])
        u1 = jnp.concatenate([f1, u1[8:]], axis=0)
        u2 = jnp.concatenate([f2, u2[8:]], axis=0)
        w = cw_ref[:, col0:col0 + cw]
        return (u2 * w[0:1] + u1 * w[1:2] + u * w[2:3]) + cb_ref[:, col0:col0 + cw]

    for c in range(D_FF // cw):
        gate = conv(c * cw)
        val = conv(D_FF + c * cw)
        act = (_gelu_tanh(gate) * val).astype(BF16)
        acc_sc[...] += _dot(act, wdn_ref[c * cw:(c + 1) * cw, :])

    out_ref[...] = h + rms(acc_sc[...], gffn_ref[...])


def _out_and_ffn(x2, y_rwkv, y_attn, w_out, g_post, g_pre, w_up, conv_w, conv_b, w_down, g_ffn):
    S = x2.shape[0]
    tm = FFN_ROWS
    row = lambda z: z.reshape(1, -1).astype(F32)
    single = functools.partial(pl.BlockSpec, pipeline_mode=pl.Buffered(1))
    const1 = lambda shape: single(shape, lambda i: (0,) * len(shape))
    return pl.pallas_call(
        _ffn_kernel,
        grid=(S // tm,),
        in_specs=[
            pl.BlockSpec((tm, D_MODEL), lambda i: (i, 0)),
            pl.BlockSpec((tm, RWKV_DIM), lambda i: (i, 0)),
            pl.BlockSpec((tm, DIFF_DIM), lambda i: (i, 0)),
            const1((D_MODEL, D_MODEL)),
            _const_spec((1, D_MODEL)), _const_spec((1, D_MODEL)),
            const1((D_MODEL, 2 * D_FF)),
            _const_spec((CONV_WIDTH, 2 * D_FF)), _const_spec((1, 2 * D_FF)),
            const1((D_FF, D_MODEL)),
            _const_spec((1, D_MODEL)),
        ],
        out_specs=pl.BlockSpec((tm, D_MODEL), lambda i: (i, 0)),
        out_shape=jax.ShapeDtypeStruct((S, D_MODEL), F32),
        scratch_shapes=[
            pltpu.VMEM((8, 2 * D_FF), F32),
            pltpu.VMEM((tm, D_MODEL), BF16),
            pltpu.VMEM((tm, D_MODEL), F32),
        ],
        compiler_params=pltpu.CompilerParams(
            dimension_semantics=("arbitrary",), vmem_limit_bytes=VMEM_LIMIT),
        name="out_ffn",
    )(x2, y_rwkv, y_attn, w_out.astype(BF16), row(g_post), row(g_pre), w_up.astype(BF16),
      conv_w.astype(F32), row(conv_b), w_down.astype(BF16), row(g_ffn))


def kernel(x, ln_attn_pre, w_in, mu_shift, w_decay_up, w_decay0, w_iclr_up, w_iclr0, w_gate_up,
           k_k, k_a, r_k, ln_x_w, ln_x_b, lambda_q1, lambda_k1, lambda_q2, lambda_k2, diff_subln,
           w_out, ln_attn_post, ln_ffn_pre, w_up, conv_w, conv_b, w_down, ln_ffn_post):
    B, S, _ = x.shape
    assert B == 1 and w_in.shape[0] == 1, "single batch, depth 1"
    assert S % POS_BLOCK == 0
    x2 = x[0]
    rw, qp, kp, vt = _in_projection(x2, ln_attn_pre[0], w_in[0])
    y_rwkv = _rwkv_mix(rw, mu_shift[0], w_decay_up[0], w_decay0[0], w_iclr_up[0], w_iclr0[0],
                       w_gate_up[0], k_k[0], k_a[0], r_k[0], ln_x_w[0], ln_x_b[0])
    y_attn = _diff_attention(qp, kp, vt, lambda_q1[0], lambda_k1[0], lambda_q2[0], lambda_k2[0],
                             diff_subln[0])
    out = _out_and_ffn(x2, y_rwkv, y_attn, w_out[0], ln_attn_post[0], ln_ffn_pre[0], w_up[0],
                       conv_w[0], conv_b[0], w_down[0], ln_ffn_post[0])
    return out[None]
```

```python
import functools
import math

import jax
import jax.numpy as jnp
import numpy as np
from jax import lax
from jax.experimental import pallas as pl
from jax.experimental.pallas import tpu as pltpu

F32 = jnp.float32
BF16 = jnp.bfloat16

D_MODEL = 1024
RWKV_HEADS = 8
RWKV_HEAD_DIM = 64
RWKV_DIM = RWKV_HEADS * RWKV_HEAD_DIM
LORA_W = 64
LORA_A = 64
LORA_G = 128
DIFF_HEADS = 4
DIFF_HEAD_DIM = 64
DIFF_DIM = DIFF_HEADS * 2 * DIFF_HEAD_DIM
RWKV_COLS = 3 * RWKV_DIM + LORA_W + LORA_A + LORA_G
D_FF = 2816
CONV_WIDTH = 3
DECAY_SCALE = math.exp(-0.5)
RWKV_LN_EPS = 64e-5
NORM_EPS = 1e-6
SUBLN_EPS = 1e-5
NEG_INF = -1e30
LAM_INIT = 0.8 - 0.6 * math.exp(-0.3 * 0)

LANES = 128
VMEM_LIMIT = 56 * 1024 * 1024

ATT_TQ = 1024
ATT_TK = 512
ATT_QB = 256
ATT_LOOKAHEAD = 3
POS_BLOCK = 1024
POS_SPLIT = 256
N_MAPS = 2 * DIFF_HEADS
QK_PAD = 128
VT_ROWS = 144

CHUNK = 64
RWKV_ROWS = 256
N_PAIRS = RWKV_HEADS // 2

PROJ_ROWS = 512
FFN_ROWS = 512
FFN_CHUNK = 256


def _dot(a, b):
    return jnp.dot(a, b, preferred_element_type=F32)


def _dot_nt(a, b):
    return lax.dot_general(a, b, (((1,), (1,)), ((), ())), preferred_element_type=F32)


def _dot_tn(a, b):
    return lax.dot_general(a, b, (((0,), (0,)), ((), ())), preferred_element_type=F32)


def _const_spec(shape):
    nd = len(shape)
    return pl.BlockSpec(shape, lambda *_: (0,) * nd)


def _inproj_kernel(x_ref, g_ref, w_ref, wvt_ref, qpat_ref, kpat_ref, vtpat_ref,
                   rw_ref, q_ref, k_ref, vt_ref):
    x = x_ref[...]
    ms = jnp.mean(x * x, axis=-1, keepdims=True)
    xn = (x * lax.rsqrt(ms + NORM_EPS) * g_ref[...]).astype(BF16)
    c0 = RWKV_COLS
    c1 = c0 + N_MAPS * QK_PAD
    c2 = c1 + N_MAPS * QK_PAD
    rw_ref[...] = _dot(xn, w_ref[:, 0:c0])
    q_ref[...] = (_dot(xn, w_ref[:, c0:c1]) + qpat_ref[...]).astype(BF16)
    k_ref[...] = (_dot(xn, w_ref[:, c1:c2]) + kpat_ref[...]).astype(BF16)
    vt = (_dot_nt(wvt_ref[...], xn) + vtpat_ref[...]).astype(BF16)
    for h in range(DIFF_HEADS):
        vt_ref[h, 0] = vt[h * VT_ROWS:(h + 1) * VT_ROWS]


def _alibi_slopes():
    return [2.0 ** (-8.0 * (i + 1) / DIFF_HEADS) for i in range(DIFF_HEADS)]


def _alibi_patterns():
    pos = np.arange(POS_BLOCK)
    lo = (pos % POS_SPLIT).astype(np.float32)
    hi = (pos - pos % POS_SPLIT).astype(np.float32)
    qpat = np.zeros((POS_BLOCK, N_MAPS, QK_PAD), np.float32)
    kpat = np.zeros((POS_BLOCK, N_MAPS, QK_PAD), np.float32)
    d = DIFF_HEAD_DIM
    for m in range(N_MAPS):
        slope = _alibi_slopes()[m // 2]
        qpat[:, m, d + 0] = -slope * lo
        qpat[:, m, d + 1] = -slope * hi
        qpat[:, m, d + 2:d + 4] = 1.0
        kpat[:, m, d + 0:d + 2] = 1.0
        kpat[:, m, d + 2] = slope * lo
        kpat[:, m, d + 3] = slope * hi
    vtpat = np.zeros((DIFF_HEADS, VT_ROWS, PROJ_ROWS), np.float32)
    vtpat[:, 2 * d, :] = 1.0
    return (jnp.asarray(qpat.reshape(POS_BLOCK, N_MAPS * QK_PAD), BF16),
            jnp.asarray(kpat.reshape(POS_BLOCK, N_MAPS * QK_PAD), BF16),
            jnp.asarray(vtpat.reshape(DIFF_HEADS * VT_ROWS, PROJ_ROWS), BF16))


def _pack_w_in(w_in):
    d = DIFF_HEAD_DIM
    w_rw = w_in[:, :RWKV_COLS]
    wq = w_in[:, RWKV_COLS:RWKV_COLS + DIFF_DIM].reshape(D_MODEL, N_MAPS, d) * (d ** -0.5)
    wk = w_in[:, RWKV_COLS + DIFF_DIM:RWKV_COLS + 2 * DIFF_DIM].reshape(D_MODEL, N_MAPS, d)
    wv = w_in[:, RWKV_COLS + 2 * DIFF_DIM:].reshape(D_MODEL, DIFF_HEADS, 2 * d)
    wq = jnp.pad(wq, ((0, 0), (0, 0), (0, QK_PAD - d))).reshape(D_MODEL, N_MAPS * QK_PAD)
    wk = jnp.pad(wk, ((0, 0), (0, 0), (0, QK_PAD - d))).reshape(D_MODEL, N_MAPS * QK_PAD)
    wvt = jnp.pad(wv.transpose(1, 2, 0), ((0, 0), (0, VT_ROWS - 2 * d), (0, 0)))
    return (jnp.concatenate([w_rw, wq, wk], axis=1).astype(BF16),
            wvt.reshape(DIFF_HEADS * VT_ROWS, D_MODEL).astype(BF16))


def _in_projection(x2, g, w_in):
    S = x2.shape[0]
    tm = PROJ_ROWS
    assert tm == ATT_TK, "the transposed value blocks are consumed one per attention kv step"
    w, wvt = _pack_w_in(w_in)
    qpat, kpat, vtpat = _alibi_patterns()
    nq = N_MAPS * QK_PAD
    rep = POS_BLOCK // tm
    single = functools.partial(pl.BlockSpec, pipeline_mode=pl.Buffered(1))
    return pl.pallas_call(
        _inproj_kernel,
        grid=(S // tm,),
        in_specs=[
            pl.BlockSpec((tm, D_MODEL), lambda i: (i, 0)),
            _const_spec((1, D_MODEL)),
            single(w.shape, lambda i: (0, 0)),
            single(wvt.shape, lambda i: (0, 0)),
            pl.BlockSpec((tm, nq), lambda i: (i % rep, 0)),
            pl.BlockSpec((tm, nq), lambda i: (i % rep, 0)),
            _const_spec(vtpat.shape),
        ],
        out_specs=[
            pl.BlockSpec((tm, RWKV_COLS), lambda i: (i, 0)),
            pl.BlockSpec((tm, nq), lambda i: (i, 0)),
            pl.BlockSpec((tm, nq), lambda i: (i, 0)),
            pl.BlockSpec((DIFF_HEADS, 1, VT_ROWS, tm), lambda i: (0, i, 0, 0)),
        ],
        out_shape=[
            jax.ShapeDtypeStruct((S, RWKV_COLS), F32),
            jax.ShapeDtypeStruct((S, nq), BF16),
            jax.ShapeDtypeStruct((S, nq), BF16),
            jax.ShapeDtypeStruct((DIFF_HEADS, S // tm, VT_ROWS, tm), BF16),
        ],
        compiler_params=pltpu.CompilerParams(
            dimension_semantics=("arbitrary",), vmem_limit_bytes=VMEM_LIMIT),
        name="in_projection",
    )(x2, g.reshape(1, D_MODEL), w, wvt, qpat, kpat, vtpat)


def _rwkv_kernel(rw_ref, mu_ref, wwa_ref, wd0_ref, wa0_ref, wg_ref, kk_ref, ka_ref, rk_ref,
                 lnw_ref, lnb_ref, hsum_ref, cum_ref, y_ref, prev_sc, state_sc, ych_sc):
    T = RWKV_ROWS
    C = CHUNK
    D = RWKV_DIM

    @pl.when(pl.program_id(0) == 0)
    def _():
        prev_sc[...] = jnp.zeros(prev_sc.shape, F32)
        state_sc[...] = jnp.zeros(state_sc.shape, F32)

    p = rw_ref[...]
    row = lax.broadcasted_iota(jnp.int32, (8, RWKV_COLS), 0)
    shifted = pltpu.roll(p, 1, axis=0)
    first = jnp.where(row == 0, prev_sc[...], shifted[0:8])
    shifted = jnp.concatenate([first, shifted[8:]], axis=0)
    prev_sc[...] = jnp.broadcast_to(p[T - 1:T, :], prev_sc.shape)
    xs = p + (shifted - p) * mu_ref[...]

    r = xs[:, 0:D]
    k = xs[:, D:2 * D]
    v = xs[:, 2 * D:3 * D]
    wa = xs[:, 3 * D:3 * D + LANES]
    g_lo = xs[:, 3 * D + LANES:3 * D + 2 * LANES]

    lane = lax.broadcasted_iota(jnp.int32, (T, LANES), 1)
    wa_act = jnp.where(lane < LORA_W, jnp.tanh(wa), wa).astype(BF16)
    pre = _dot(wa_act, wwa_ref[...])
    lw = -DECAY_SCALE * jax.nn.sigmoid(wd0_ref[...] + pre[:, 0:D])
    a = jax.nn.sigmoid(wa0_ref[...] + pre[:, D:2 * D])
    g = _dot(jax.nn.sigmoid(g_lo).astype(BF16), wg_ref[...])

    hsum = hsum_ref[...]
    kk = k * kk_ref[...]
    n2 = _dot((kk * kk).astype(BF16), hsum)
    kk = kk * jnp.minimum(lax.rsqrt(n2), 1e12)
    k2 = k * (1.0 + (a - 1.0) * ka_ref[...])
    bonus = _dot((r * k2 * rk_ref[...]).astype(BF16), hsum)

    lw_hi = lw.astype(BF16)
    lw_lo = (lw - lw_hi.astype(F32)).astype(BF16)
    cum = _dot(cum_ref[...], lw_hi) + _dot(cum_ref[...], lw_lo)
    L = cum[0:T]
    LC = cum[T:2 * T]
    e_fwd = jnp.exp(L)
    e_prev = jnp.exp(L - lw)
    e_inv = jnp.exp(-L)
    e_end = jnp.exp(LC - L)
    p_end = jnp.exp(LC)
    kka = kk * a
    r_t = (r * e_fwd).astype(BF16)
    al_t = (-kk * e_prev)
    k_t = (k2 * e_inv).astype(BF16)
    be_t = (kka * e_inv).astype(BF16)
    k_e = (k2 * e_end).astype(BF16)
    be_e = (kka * e_end).astype(BF16)
    v_b = v.astype(BF16)

    rr = lax.broadcasted_iota(jnp.int32, (2 * C, LANES), 0)
    ll = lax.broadcasted_iota(jnp.int32, (2 * C, LANES), 1)
    bd_mask = (rr < C) == (ll < C)
    tri_r = lax.broadcasted_iota(jnp.int32, (C, LANES), 0)
    tri_l = lax.broadcasted_iota(jnp.int32, (C, LANES), 1) & (C - 1)
    strict = tri_l < tri_r
    incl = tri_l <= tri_r

    def bd(z):
        zb = z.astype(BF16)
        return jnp.where(bd_mask, jnp.concatenate([zb, zb], axis=0), jnp.zeros((), BF16))

    n_chunks = T // C
    pairs = range(N_PAIRS)
    blocks = [(c, pr) for c in range(n_chunks) for pr in pairs]
    blk = {(c, pr): (slice(c * C, (c + 1) * C), slice(pr * LANES, (pr + 1) * LANES))
           for c, pr in blocks}

    gram = {}
    for key in blocks:
        sl = blk[key]
        xs_c = jnp.concatenate([al_t[sl].astype(BF16), r_t[sl]], axis=0)
        y_bd = jnp.concatenate([bd(be_t[sl]), bd(k_t[sl])], axis=0)
        gram[key] = _dot_nt(xs_c, y_bd)
    a_pow = {key: jnp.where(strict, gram[key][0:C, 0:LANES], 0.0) for key in blocks}
    a_r = {key: jnp.concatenate([jnp.where(incl, gram[key][C:2 * C, 0:LANES], 0.0),
                                 jnp.where(incl, gram[key][C:2 * C, LANES:2 * LANES], 0.0)],
                                axis=1).astype(BF16) for key in blocks}
    v_bd = {key: bd(v_b[blk[key]]) for key in blocks}

    x_u = {key: al_t[blk[key]] for key in blocks}
    x_0 = {key: _dot(jnp.where(strict, gram[key][0:C, LANES:2 * LANES], 0.0).astype(BF16), v_bd[key])
           for key in blocks}
    n_steps = int(math.log2(C))
    for step in range(n_steps):
        a_pow_b = {key: a_pow[key].astype(BF16) for key in blocks}
        for key in blocks:
            rhs = jnp.concatenate([bd(x_u[key]), bd(x_0[key])], axis=1)
            upd = _dot(a_pow_b[key], rhs)
            x_u[key] = x_u[key] + upd[:, 0:LANES]
            x_0[key] = x_0[key] + upd[:, LANES:2 * LANES]
        if step + 1 < n_steps:
            a_pow = {key: _dot(a_pow_b[key], bd(a_pow[key])) for key in blocks}

    st = {pr: state_sc[pr] for pr in pairs}
    for c in range(n_chunks):
        st_b = {pr: st[pr].astype(BF16) for pr in pairs}
        u = {pr: _dot_nt(x_u[c, pr].astype(BF16), st_b[pr]) + x_0[c, pr] for pr in pairs}
        for pr in pairs:
            sl = blk[c, pr]
            vu = jnp.concatenate([v_b[sl], u[pr].astype(BF16)], axis=0)
            kb = jnp.concatenate([k_e[sl], be_e[sl]], axis=0)
            upd = _dot_tn(vu, kb)
            st[pr] = st[pr] * p_end[c * C:c * C + 1, sl[1]] + jnp.where(bd_mask, upd, 0.0)
        y_s = {pr: _dot_nt(r_t[blk[c, pr]], st_b[pr]) for pr in pairs}
        for pr in pairs:
            ych_sc[blk[c, pr]] = y_s[pr] + _dot(
                a_r[c, pr], jnp.concatenate([bd(u[pr]), v_bd[c, pr]], axis=0))
    for pr in pairs:
        state_sc[pr] = st[pr]

    y = ych_sc[...] + bonus * v
    inv_n = 1.0 / RWKV_HEAD_DIM
    mean = _dot(y.astype(BF16), hsum) * inv_n
    yc = y - mean
    var = _dot((yc * yc).astype(BF16), hsum) * inv_n
    yn = yc * lax.rsqrt(var + RWKV_LN_EPS)
    y_ref[...] = ((yn * lnw_ref[...] + lnb_ref[...]) * g).astype(y_ref.dtype)


def _rwkv_mix(rw, mu, w_decay_up, w_decay0, w_iclr_up, w_iclr0, w_gate_up, k_k, k_a, r_k,
              ln_x_w, ln_x_b):
    S = rw.shape[0]
    T = RWKV_ROWS
    D = RWKV_DIM
    wwa = jnp.zeros((LANES, 2 * D), F32)
    wwa = wwa.at[0:LORA_W, 0:D].set(w_decay_up).at[LORA_W:LORA_W + LORA_A, D:2 * D].set(w_iclr_up)
    hid = jnp.arange(D) // RWKV_HEAD_DIM
    hsum = (hid[:, None] == hid[None, :]).astype(BF16)
    t = jnp.arange(T)
    same = (t[:, None] // CHUNK) == (t[None, :] // CHUNK)
    cum = jnp.concatenate([same & (t[None, :] <= t[:, None]), same], axis=0).astype(BF16)
    row = lambda z: z.reshape(1, -1).astype(F32)
    args = (rw, row(mu), wwa.astype(BF16), row(w_decay0), row(w_iclr0), w_gate_up.astype(BF16),
            row(k_k), row(k_a), row(r_k), row(ln_x_w), row(ln_x_b), hsum, cum)
    in_specs = [pl.BlockSpec((T, RWKV_COLS), lambda i: (i, 0))]
    in_specs += [_const_spec(z.shape) for z in args[1:]]
    return pl.pallas_call(
        _rwkv_kernel,
        grid=(S // T,),
        in_specs=in_specs,
        out_specs=pl.BlockSpec((T, D), lambda i: (i, 0)),
        out_shape=jax.ShapeDtypeStruct((S, D), BF16),
        scratch_shapes=[
            pltpu.VMEM((8, RWKV_COLS), F32),
            pltpu.VMEM((N_PAIRS, LANES, LANES), F32),
            pltpu.VMEM((T, D), F32),
        ],
        compiler_params=pltpu.CompilerParams(
            dimension_semantics=("arbitrary",), vmem_limit_bytes=VMEM_LIMIT),
        name="rwkv7_mix",
    )(*args)


def _attn_kernel(slopes_ref, q_ref, k_ref, vt_ref, lq1_ref, lk1_ref, lq2_ref, lk2_ref, sub_ref,
                 o_ref, acc_sc, s_sc):
    tq = ATT_TQ
    tk = ATT_TK
    qb = ATT_QB
    nb = tq // qb
    kv_per_q = tq // tk
    h = pl.program_id(0)
    i = pl.program_id(1)
    slope = slopes_ref[h]
    q_base = ((i * tq) // POS_BLOCK) * POS_BLOCK

    acc_sc[...] = jnp.zeros(acc_sc.shape, F32)
    chains = [(c, b) for c in range(2) for b in range(nb)]
    qs = [q_ref[b * qb:(b + 1) * qb, c * QK_PAD:(c + 1) * QK_PAD] for c, b in chains]
    n = len(chains)
    ahead = ATT_LOOKAHEAD

    def scores(j, ci, rows):
        c, _ = chains[ci]
        k0 = pl.multiple_of(j * tk, tk)
        kc = k_ref[pl.ds(k0, rows), c * QK_PAD:(c + 1) * QK_PAD]
        return _dot_nt(kc, qs[ci])

    def accumulate(j, ci, s, m_old, kv_offset):
        c, b = chains[ci]
        rows = s.shape[0]
        k_base = ((j * tk) // POS_BLOCK) * POS_BLOCK
        shift = slope * (q_base - k_base).astype(F32)
        if kv_offset is not None:
            kv_i = lax.broadcasted_iota(jnp.int32, (rows, qb), 0) + kv_offset
            q_i = lax.broadcasted_iota(jnp.int32, (rows, qb), 1) + b * qb
            s = jnp.where(kv_i <= q_i, s, NEG_INF)
        m_new = jnp.maximum(m_old, jnp.max(s, axis=0, keepdims=True) - shift)
        p = jnp.exp(s - (m_new + shift)).astype(BF16)
        alpha = jnp.exp(m_old - m_new)
        cols = slice(b * qb, (b + 1) * qb)
        acc_sc[c, :, cols] = alpha * acc_sc[c, :, cols] + _dot(vt_ref[j, :, 0:rows], p)
        return m_new

    def run(tasks, ms, next_tile):
        ms = list(ms)
        early = {}
        for t, (j, ci, rows, kv_offset) in enumerate(tasks):
            nxt = t + ahead
            if nxt < len(tasks):
                early[nxt] = scores(tasks[nxt][0], tasks[nxt][1], tasks[nxt][2])
            elif next_tile is not None:
                s_sc[nxt - len(tasks)] = scores(next_tile, nxt - len(tasks), tk)
            s = s_sc[t, 0:rows] if t < ahead else early.pop(t)
            ms[ci] = accumulate(j, ci, s, ms[ci], kv_offset)
        return tuple(ms)

    for t in range(ahead):
        s_sc[t] = scores(0, t, tk)
    ms0 = tuple(jnp.full((1, qb), NEG_INF, F32) for _ in chains)
    first_diag = i * kv_per_q
    ms = lax.fori_loop(
        0, i,
        lambda g, ms: run([(g * kv_per_q + d, ci, tk, None) for d in range(kv_per_q) for ci in range(n)],
                          ms, (g + 1) * kv_per_q), ms0)

    tasks = []
    for d in range(kv_per_q):
        for ci, (c, b) in enumerate(chains):
            rows = min(tk, (b + 1) * qb - d * tk)
            if rows > 0:
                crosses = d * tk + rows - 1 > b * qb
                tasks.append((first_diag + d, ci, rows, d * tk if crosses else None))
    run(tasks, ms, None)

    d2 = 2 * DIFF_HEAD_DIM
    lam = (jnp.exp(jnp.sum(lq1_ref[...] * lk1_ref[...], axis=-1, keepdims=True))
           - jnp.exp(jnp.sum(lq2_ref[...] * lk2_ref[...], axis=-1, keepdims=True)) + LAM_INIT)
    a0 = acc_sc[0]
    a1 = acc_sc[1]
    o = a0[0:d2] / a0[d2:d2 + 1] - lam * (a1[0:d2] / a1[d2:d2 + 1])
    ms_o = jnp.mean(o * o, axis=0, keepdims=True)
    o = o * lax.rsqrt(ms_o + SUBLN_EPS) * sub_ref[...] * (1.0 - LAM_INIT)
    o_ref[...] = o.T.astype(o_ref.dtype)


def _diff_attention(qp, kp, vt, lq1, lk1, lq2, lk2, subln):
    S = qp.shape[0]
    tq = ATT_TQ
    d2 = 2 * DIFF_HEAD_DIM
    slopes = jnp.array(_alibi_slopes(), F32)
    vec = lambda z: z.reshape(1, -1).astype(F32)
    resident = functools.partial(pl.BlockSpec, pipeline_mode=pl.Buffered(1))
    return pl.pallas_call(
        _attn_kernel,
        grid=(DIFF_HEADS, S // tq),
        in_specs=[
            pl.BlockSpec(memory_space=pltpu.SMEM),
            pl.BlockSpec((tq, 2 * QK_PAD), lambda h, i: (i, h)),
            resident((S, 2 * QK_PAD), lambda h, i: (0, h)),
            resident((None, S // ATT_TK, VT_ROWS, ATT_TK), lambda h, i: (h, 0, 0, 0)),
            _const_spec((1, DIFF_HEAD_DIM)), _const_spec((1, DIFF_HEAD_DIM)),
            _const_spec((1, DIFF_HEAD_DIM)), _const_spec((1, DIFF_HEAD_DIM)),
            _const_spec((d2, 1)),
        ],
        out_specs=pl.BlockSpec((tq, d2), lambda h, i: (i, h)),
        out_shape=jax.ShapeDtypeStruct((S, DIFF_DIM), BF16),
        scratch_shapes=[pltpu.VMEM((2, VT_ROWS, tq), F32),
                        pltpu.VMEM((ATT_LOOKAHEAD, ATT_TK, ATT_QB), F32)],
        compiler_params=pltpu.CompilerParams(
            dimension_semantics=("arbitrary", "arbitrary"), vmem_limit_bytes=VMEM_LIMIT),
        name="diff_attention",
    )(slopes, qp, kp, vt, vec(lq1), vec(lk1), vec(lq2), vec(lk2),
      subln.reshape(d2, 1).astype(F32))


def _gelu_tanh(x):
    return 0.5 * x * (1.0 + jnp.tanh(math.sqrt(2.0 / math.pi) * (x + 0.044715 * (x * x * x))))


def _ffn_kernel(x_ref, yr_ref, yo_ref, wo_ref, gpost_ref, gpre_ref, wup_ref, cw_ref, cb_ref,
                wdn_ref, gffn_ref, out_ref, tail_sc, hn_sc, acc_sc):
    tm = FFN_ROWS
    cw = FFN_CHUNK

    @pl.when(pl.program_id(0) == 0)
    def _():
        tail_sc[...] = jnp.zeros(tail_sc.shape, F32)

    def rms(z, g_row):
        ms = jnp.mean(z * z, axis=-1, keepdims=True)
        return z * lax.rsqrt(ms + NORM_EPS) * g_row

    half = RWKV_DIM
    mixed = _dot(yr_ref[...], wo_ref[0:half, :]) + _dot(yo_ref[...], wo_ref[half:2 * half, :])
    h = x_ref[...] + rms(mixed, gpost_ref[...])
    hn_sc[...] = rms(h, gpre_ref[...]).astype(BF16)
    acc_sc[...] = jnp.zeros(acc_sc.shape, F32)

    row8 = lax.broadcasted_iota(jnp.int32, (8, cw), 0)

    def up(col0):
        return _dot(hn_sc[...], wup_ref[:, col0:col0 + cw])

    def conv(u, col0):
        tail = tail_sc[:, col0:col0 + cw]
        tail_sc[:, col0:col0 + cw] = u[tm - 8:tm]
        u1 = pltpu.roll(u, 1, axis=0)
        u2 = pltpu.roll(u, 2, axis=0)
        f1 = jnp.where(row8 < 1, pltpu.roll(tail, 1, axis=0), u1[0:8])
        f2 = jnp.where(row8 < 2, pltpu.roll(tail, 2, axis=0), u2[0:8])
        u1 = jnp.concatenate([f1, u1[8:]], axis=0)
        u2 = jnp.concatenate([f2, u2[8:]], axis=0)
        w = cw_ref[:, col0:col0 + cw]
        return (u2 * w[0:1] + u1 * w[1:2] + u * w[2:3]) + cb_ref[:, col0:col0 + cw]

    n_chunks = D_FF // cw
    nxt = (up(0), up(D_FF))
    for c in range(n_chunks):
        u_gate, u_val = nxt
        if c + 1 < n_chunks:
            nxt = (up((c + 1) * cw), up(D_FF + (c + 1) * cw))
        gate = conv(u_gate, c * cw)
        val = conv(u_val, D_FF + c * cw)
        act = (_gelu_tanh(gate) * val).astype(BF16)
        acc_sc[...] += _dot(act, wdn_ref[c * cw:(c + 1) * cw, :])

    out_ref[...] = h + rms(acc_sc[...], gffn_ref[...])


def _out_and_ffn(x2, y_rwkv, y_attn, w_out, g_post, g_pre, w_up, conv_w, conv_b, w_down, g_ffn):
    S = x2.shape[0]
    tm = FFN_ROWS
    row = lambda z: z.reshape(1, -1).astype(F32)
    single = functools.partial(pl.BlockSpec, pipeline_mode=pl.Buffered(1))
    const1 = lambda shape: single(shape, lambda i: (0,) * len(shape))
    return pl.pallas_call(
        _ffn_kernel,
        grid=(S // tm,),
        in_specs=[
            pl.BlockSpec((tm, D_MODEL), lambda i: (i, 0)),
            pl.BlockSpec((tm, RWKV_DIM), lambda i: (i, 0)),
            pl.BlockSpec((tm, DIFF_DIM), lambda i: (i, 0)),
            const1((D_MODEL, D_MODEL)),
            _const_spec((1, D_MODEL)), _const_spec((1, D_MODEL)),
            const1((D_MODEL, 2 * D_FF)),
            _const_spec((CONV_WIDTH, 2 * D_FF)), _const_spec((1, 2 * D_FF)),
            const1((D_FF, D_MODEL)),
            _const_spec((1, D_MODEL)),
        ],
        out_specs=pl.BlockSpec((tm, D_MODEL), lambda i: (i, 0)),
        out_shape=jax.ShapeDtypeStruct((S, D_MODEL), F32),
        scratch_shapes=[
            pltpu.VMEM((8, 2 * D_FF), F32),
            pltpu.VMEM((tm, D_MODEL), BF16),
            pltpu.VMEM((tm, D_MODEL), F32),
        ],
        compiler_params=pltpu.CompilerParams(
            dimension_semantics=("arbitrary",), vmem_limit_bytes=VMEM_LIMIT),
        name="out_ffn",
    )(x2, y_rwkv, y_attn, w_out.astype(BF16), row(g_post), row(g_pre), w_up.astype(BF16),
      conv_w.astype(F32), row(conv_b), w_down.astype(BF16), row(g_ffn))


def kernel(x, ln_attn_pre, w_in, mu_shift, w_decay_up, w_decay0, w_iclr_up, w_iclr0, w_gate_up,
           k_k, k_a, r_k, ln_x_w, ln_x_b, lambda_q1, lambda_k1, lambda_q2, lambda_k2, diff_subln,
           w_out, ln_attn_post, ln_ffn_pre, w_up, conv_w, conv_b, w_down, ln_ffn_post):
    B, S, _ = x.shape
    assert B == 1 and w_in.shape[0] == 1, "single batch, depth 1"
    assert S % POS_BLOCK == 0
    x2 = x[0]
    rw, qp, kp, vt = _in_projection(x2, ln_attn_pre[0], w_in[0])
    y_rwkv = _rwkv_mix(rw, mu_shift[0], w_decay_up[0], w_decay0[0], w_iclr_up[0], w_iclr0[0],
                       w_gate_up[0], k_k[0], k_a[0], r_k[0], ln_x_w[0], ln_x_b[0])
    y_attn = _diff_attention(qp, kp, vt, lambda_q1[0], lambda_k1[0], lambda_q2[0], lambda_k2[0],
                             diff_subln[0])
    out = _out_and_ffn(x2, y_rwkv, y_attn, w_out[0], ln_attn_post[0], ln_ffn_pre[0], w_up[0],
                       conv_w[0], conv_b[0], w_down[0], ln_ffn_post[0])
    return out[None]
```

```python
import functools
import math

import jax
import jax.numpy as jnp
import numpy as np
from jax import lax
from jax.experimental import pallas as pl
from jax.experimental.pallas import tpu as pltpu

F32 = jnp.float32
BF16 = jnp.bfloat16

D_MODEL = 1024
RWKV_HEADS = 8
RWKV_HEAD_DIM = 64
RWKV_DIM = RWKV_HEADS * RWKV_HEAD_DIM
LORA_W = 64
LORA_A = 64
LORA_G = 128
DIFF_HEADS = 4
DIFF_HEAD_DIM = 64
DIFF_DIM = DIFF_HEADS * 2 * DIFF_HEAD_DIM
RWKV_COLS = 3 * RWKV_DIM + LORA_W + LORA_A + LORA_G
D_FF = 2816
CONV_WIDTH = 3
DECAY_SCALE = math.exp(-0.5)
RWKV_LN_EPS = 64e-5
NORM_EPS = 1e-6
SUBLN_EPS = 1e-5
NEG_INF = -1e30
LAM_INIT = 0.8 - 0.6 * math.exp(-0.3 * 0)

LANES = 128
VMEM_LIMIT = 56 * 1024 * 1024

ATT_TQ = 1024
ATT_TK = 512
ATT_QB = 256
ATT_LOOKAHEAD = 3
POS_BLOCK = 1024
POS_SPLIT = 256
N_MAPS = 2 * DIFF_HEADS
QK_PAD = 128
VT_ROWS = 144

CHUNK = 64
RWKV_ROWS = 256
N_PAIRS = RWKV_HEADS // 2

PROJ_ROWS = 512
FFN_ROWS = 512
FFN_CHUNK = 256


def _dot(a, b):
    return jnp.dot(a, b, preferred_element_type=F32)


def _dot_nt(a, b):
    return lax.dot_general(a, b, (((1,), (1,)), ((), ())), preferred_element_type=F32)


def _dot_tn(a, b):
    return lax.dot_general(a, b, (((0,), (0,)), ((), ())), preferred_element_type=F32)


def _const_spec(shape):
    nd = len(shape)
    return pl.BlockSpec(shape, lambda *_: (0,) * nd)


def _inproj_kernel(x_ref, g_ref, mu_ref, w_ref, wvt_ref, qpat_ref, kpat_ref, vtpat_ref,
                   rw_ref, q_ref, k_ref, vt_ref, prev_sc):
    tm = PROJ_ROWS

    @pl.when(pl.program_id(0) == 0)
    def _():
        prev_sc[...] = jnp.zeros(prev_sc.shape, F32)

    x = x_ref[...]
    ms = jnp.mean(x * x, axis=-1, keepdims=True)
    xn = (x * lax.rsqrt(ms + NORM_EPS) * g_ref[...]).astype(BF16)
    c0 = RWKV_COLS
    c1 = c0 + N_MAPS * QK_PAD
    c2 = c1 + N_MAPS * QK_PAD
    p = _dot(xn, w_ref[:, 0:c0])
    row = lax.broadcasted_iota(jnp.int32, (8, RWKV_COLS), 0)
    shifted = pltpu.roll(p, 1, axis=0)
    first = jnp.where(row == 0, prev_sc[...], shifted[0:8])
    shifted = jnp.concatenate([first, shifted[8:]], axis=0)
    prev_sc[...] = jnp.broadcast_to(p[tm - 1:tm, :], prev_sc.shape)
    rw_ref[...] = p + (shifted - p) * mu_ref[...]
    q_ref[...] = (_dot(xn, w_ref[:, c0:c1]) + qpat_ref[...]).astype(BF16)
    k_ref[...] = (_dot(xn, w_ref[:, c1:c2]) + kpat_ref[...]).astype(BF16)
    vt = (_dot_nt(wvt_ref[...], xn) + vtpat_ref[...]).astype(BF16)
    for h in range(DIFF_HEADS):
        vt_ref[h, 0] = vt[h * VT_ROWS:(h + 1) * VT_ROWS]


def _alibi_slopes():
    return [2.0 ** (-8.0 * (i + 1) / DIFF_HEADS) for i in range(DIFF_HEADS)]


def _alibi_patterns():
    pos = np.arange(POS_BLOCK)
    lo = (pos % POS_SPLIT).astype(np.float32)
    hi = (pos - pos % POS_SPLIT).astype(np.float32)
    qpat = np.zeros((POS_BLOCK, N_MAPS, QK_PAD), np.float32)
    kpat = np.zeros((POS_BLOCK, N_MAPS, QK_PAD), np.float32)
    d = DIFF_HEAD_DIM
    for m in range(N_MAPS):
        slope = _alibi_slopes()[m // 2]
        qpat[:, m, d + 0] = -slope * lo
        qpat[:, m, d + 1] = -slope * hi
        qpat[:, m, d + 2:d + 4] = 1.0
        kpat[:, m, d + 0:d + 2] = 1.0
        kpat[:, m, d + 2] = slope * lo
        kpat[:, m, d + 3] = slope * hi
    vtpat = np.zeros((DIFF_HEADS, VT_ROWS, PROJ_ROWS), np.float32)
    vtpat[:, 2 * d, :] = 1.0
    return (jnp.asarray(qpat.reshape(POS_BLOCK, N_MAPS * QK_PAD), BF16),
            jnp.asarray(kpat.reshape(POS_BLOCK, N_MAPS * QK_PAD), BF16),
            jnp.asarray(vtpat.reshape(DIFF_HEADS * VT_ROWS, PROJ_ROWS), BF16))


def _pack_w_in(w_in):
    d = DIFF_HEAD_DIM
    w_rw = w_in[:, :RWKV_COLS]
    wq = w_in[:, RWKV_COLS:RWKV_COLS + DIFF_DIM].reshape(D_MODEL, N_MAPS, d) * (d ** -0.5)
    wk = w_in[:, RWKV_COLS + DIFF_DIM:RWKV_COLS + 2 * DIFF_DIM].reshape(D_MODEL, N_MAPS, d)
    wv = w_in[:, RWKV_COLS + 2 * DIFF_DIM:].reshape(D_MODEL, DIFF_HEADS, 2 * d)
    wq = jnp.pad(wq, ((0, 0), (0, 0), (0, QK_PAD - d))).reshape(D_MODEL, N_MAPS * QK_PAD)
    wk = jnp.pad(wk, ((0, 0), (0, 0), (0, QK_PAD - d))).reshape(D_MODEL, N_MAPS * QK_PAD)
    wvt = jnp.pad(wv.transpose(1, 2, 0), ((0, 0), (0, VT_ROWS - 2 * d), (0, 0)))
    return (jnp.concatenate([w_rw, wq, wk], axis=1).astype(BF16),
            wvt.reshape(DIFF_HEADS * VT_ROWS, D_MODEL).astype(BF16))


def _in_projection(x2, g, mu, w_in):
    S = x2.shape[0]
    tm = PROJ_ROWS
    assert tm == ATT_TK, "the transposed value blocks are consumed one per attention kv step"
    w, wvt = _pack_w_in(w_in)
    qpat, kpat, vtpat = _alibi_patterns()
    nq = N_MAPS * QK_PAD
    rep = POS_BLOCK // tm
    single = functools.partial(pl.BlockSpec, pipeline_mode=pl.Buffered(1))
    return pl.pallas_call(
        _inproj_kernel,
        grid=(S // tm,),
        in_specs=[
            pl.BlockSpec((tm, D_MODEL), lambda i: (i, 0)),
            _const_spec((1, D_MODEL)),
            _const_spec((1, RWKV_COLS)),
            single(w.shape, lambda i: (0, 0)),
            single(wvt.shape, lambda i: (0, 0)),
            pl.BlockSpec((tm, nq), lambda i: (i % rep, 0)),
            pl.BlockSpec((tm, nq), lambda i: (i % rep, 0)),
            _const_spec(vtpat.shape),
        ],
        out_specs=[
            pl.BlockSpec((tm, RWKV_COLS), lambda i: (i, 0)),
            pl.BlockSpec((tm, nq), lambda i: (i, 0)),
            pl.BlockSpec((tm, nq), lambda i: (i, 0)),
            pl.BlockSpec((DIFF_HEADS, 1, VT_ROWS, tm), lambda i: (0, i, 0, 0)),
        ],
        out_shape=[
            jax.ShapeDtypeStruct((S, RWKV_COLS), F32),
            jax.ShapeDtypeStruct((S, nq), BF16),
            jax.ShapeDtypeStruct((S, nq), BF16),
            jax.ShapeDtypeStruct((DIFF_HEADS, S // tm, VT_ROWS, tm), BF16),
        ],
        scratch_shapes=[pltpu.VMEM((8, RWKV_COLS), F32)],
        compiler_params=pltpu.CompilerParams(
            dimension_semantics=("arbitrary",), vmem_limit_bytes=VMEM_LIMIT),
        name="in_projection",
    )(x2, g.reshape(1, D_MODEL), mu.reshape(1, RWKV_COLS).astype(F32), w, wvt, qpat, kpat, vtpat)


def _rwkv_kernel(xs_ref, wwa_ref, wd0_ref, wa0_ref, wg_ref, kk_ref, ka_ref, rk_ref,
                 lnw_ref, lnb_ref, hsum_ref, cum_ref, y_ref, state_sc, ych_sc):
    T = RWKV_ROWS
    C = CHUNK
    D = RWKV_DIM

    @pl.when(pl.program_id(0) == 0)
    def _():
        state_sc[...] = jnp.zeros(state_sc.shape, F32)

    r = xs_ref[:, 0:D]
    k = xs_ref[:, D:2 * D]
    v = xs_ref[:, 2 * D:3 * D]
    wa = xs_ref[:, 3 * D:3 * D + LANES]
    g_lo = xs_ref[:, 3 * D + LANES:3 * D + 2 * LANES]

    lane = lax.broadcasted_iota(jnp.int32, (T, LANES), 1)
    wa_act = jnp.where(lane < LORA_W, jnp.tanh(wa), wa).astype(BF16)
    pre = _dot(wa_act, wwa_ref[...])
    lw = -DECAY_SCALE * jax.nn.sigmoid(wd0_ref[...] + pre[:, 0:D])
    a = jax.nn.sigmoid(wa0_ref[...] + pre[:, D:2 * D])
    g = _dot(jax.nn.sigmoid(g_lo).astype(BF16), wg_ref[...])

    def head_sum(z):
        zb = z.astype(BF16)
        w = hsum_ref.shape[0]
        return jnp.concatenate([_dot(zb[:, o:o + w], hsum_ref[...]) for o in range(0, D, w)], axis=1)

    kk = k * kk_ref[...]
    n2 = head_sum(kk * kk)
    kk = kk * jnp.minimum(lax.rsqrt(n2), 1e12)
    k2 = k * (1.0 + (a - 1.0) * ka_ref[...])
    bonus = head_sum(r * k2 * rk_ref[...])

    lw_hi = lw.astype(BF16)
    lw_lo = (lw - lw_hi.astype(F32)).astype(BF16)
    cum = _dot(cum_ref[...], lw_hi) + _dot(cum_ref[...], lw_lo)
    L = cum[0:T]
    LC = cum[T:2 * T]
    e_fwd = jnp.exp(L)
    e_prev = jnp.exp(L - lw)
    e_inv = jnp.exp(-L)
    e_end = jnp.exp(LC - L)
    p_end = jnp.exp(LC)
    kka = kk * a
    r_t = (r * e_fwd).astype(BF16)
    al_t = (-kk * e_prev)
    k_t = (k2 * e_inv).astype(BF16)
    be_t = (kka * e_inv).astype(BF16)
    k_e = (k2 * e_end).astype(BF16)
    be_e = (kka * e_end).astype(BF16)
    v_b = v.astype(BF16)

    rr = lax.broadcasted_iota(jnp.int32, (2 * C, LANES), 0)
    ll = lax.broadcasted_iota(jnp.int32, (2 * C, LANES), 1)
    bd_mask = (rr < C) == (ll < C)
    tri_r = lax.broadcasted_iota(jnp.int32, (C, LANES), 0)
    tri_l = lax.broadcasted_iota(jnp.int32, (C, LANES), 1) & (C - 1)
    strict = tri_l < tri_r
    incl = tri_l <= tri_r

    def bd(z):
        zb = z.astype(BF16)
        return jnp.where(bd_mask, jnp.concatenate([zb, zb], axis=0), jnp.zeros((), BF16))

    n_chunks = T // C
    pairs = range(N_PAIRS)
    blocks = [(c, pr) for c in range(n_chunks) for pr in pairs]
    blk = {(c, pr): (slice(c * C, (c + 1) * C), slice(pr * LANES, (pr + 1) * LANES))
           for c, pr in blocks}

    gram = {}
    for key in blocks:
        sl = blk[key]
        xs_c = jnp.concatenate([al_t[sl].astype(BF16), r_t[sl]], axis=0)
        y_bd = jnp.concatenate([bd(be_t[sl]), bd(k_t[sl])], axis=0)
        gram[key] = _dot_nt(xs_c, y_bd)
    a_pow = {key: jnp.where(strict, gram[key][0:C, 0:LANES], 0.0) for key in blocks}
    a_r = {key: jnp.concatenate([jnp.where(incl, gram[key][C:2 * C, 0:LANES], 0.0),
                                 jnp.where(incl, gram[key][C:2 * C, LANES:2 * LANES], 0.0)],
                                axis=1).astype(BF16) for key in blocks}
    v_bd = {key: bd(v_b[blk[key]]) for key in blocks}

    x_u = {key: al_t[blk[key]] for key in blocks}
    x_0 = {key: _dot(jnp.where(strict, gram[key][0:C, LANES:2 * LANES], 0.0).astype(BF16), v_bd[key])
           for key in blocks}
    n_steps = int(math.log2(C))
    for step in range(n_steps):
        a_pow_b = {key: a_pow[key].astype(BF16) for key in blocks}
        for key in blocks:
            rhs = jnp.concatenate([bd(x_u[key]), bd(x_0[key])], axis=1)
            upd = _dot(a_pow_b[key], rhs)
            x_u[key] = x_u[key] + upd[:, 0:LANES]
            x_0[key] = x_0[key] + upd[:, LANES:2 * LANES]
        if step + 1 < n_steps:
            a_pow = {key: _dot(a_pow_b[key], bd(a_pow[key])) for key in blocks}

    w_mat = {}
    n_mat = {}
    for key in blocks:
        sl = blk[key]
        w_mat[key] = jnp.where(bd_mask, _dot_tn(x_u[key].astype(BF16), be_e[sl]), 0.0).astype(BF16)
        vu0 = jnp.concatenate([v_b[sl], x_0[key].astype(BF16)], axis=0)
        kb = jnp.concatenate([k_e[sl], be_e[sl]], axis=0)
        n_mat[key] = jnp.where(bd_mask, _dot_tn(vu0, kb), 0.0)

    def emit_outputs(c, u, y_s):
        for pr in pairs:
            ych_sc[blk[c, pr]] = y_s[pr] + _dot(
                a_r[c, pr], jnp.concatenate([bd(u[pr]), v_bd[c, pr]], axis=0))

    st = {pr: state_sc[pr] for pr in pairs}
    pending = None
    for c in range(n_chunks):
        st_b = {pr: st[pr].astype(BF16) for pr in pairs}
        st = {pr: st[pr] * p_end[c * C:c * C + 1, blk[c, pr][1]] + _dot(st_b[pr], w_mat[c, pr])
              + n_mat[c, pr] for pr in pairs}
        if pending is not None:
            emit_outputs(*pending)
        u = {pr: _dot_nt(x_u[c, pr].astype(BF16), st_b[pr]) + x_0[c, pr] for pr in pairs}
        y_s = {pr: _dot_nt(r_t[blk[c, pr]], st_b[pr]) for pr in pairs}
        pending = (c, u, y_s)
    emit_outputs(*pending)
    for pr in pairs:
        state_sc[pr] = st[pr]

    y = ych_sc[...] + bonus * v
    inv_n = 1.0 / RWKV_HEAD_DIM
    mean = head_sum(y) * inv_n
    yc = y - mean
    var = head_sum(yc * yc) * inv_n
    yn = yc * lax.rsqrt(var + RWKV_LN_EPS)
    y_ref[...] = ((yn * lnw_ref[...] + lnb_ref[...]) * g).astype(y_ref.dtype)


def _rwkv_mix(xs, w_decay_up, w_decay0, w_iclr_up, w_iclr0, w_gate_up, k_k, k_a, r_k,
              ln_x_w, ln_x_b):
    S = xs.shape[0]
    T = RWKV_ROWS
    D = RWKV_DIM
    wwa = jnp.zeros((LANES, 2 * D), F32)
    wwa = wwa.at[0:LORA_W, 0:D].set(w_decay_up).at[LORA_W:LORA_W + LORA_A, D:2 * D].set(w_iclr_up)
    hid = jnp.arange(2 * LANES) // RWKV_HEAD_DIM
    hsum = (hid[:, None] == hid[None, :]).astype(BF16)
    t = jnp.arange(T)
    same = (t[:, None] // CHUNK) == (t[None, :] // CHUNK)
    cum = jnp.concatenate([same & (t[None, :] <= t[:, None]), same], axis=0).astype(BF16)
    row = lambda z: z.reshape(1, -1).astype(F32)
    args = (xs, wwa.astype(BF16), row(w_decay0), row(w_iclr0), w_gate_up.astype(BF16),
            row(k_k), row(k_a), row(r_k), row(ln_x_w), row(ln_x_b), hsum, cum)
    in_specs = [pl.BlockSpec((T, RWKV_COLS), lambda i: (i, 0))]
    in_specs += [_const_spec(z.shape) for z in args[1:]]
    return pl.pallas_call(
        _rwkv_kernel,
        grid=(S // T,),
        in_specs=in_specs,
        out_specs=pl.BlockSpec((T, D), lambda i: (i, 0)),
        out_shape=jax.ShapeDtypeStruct((S, D), BF16),
        scratch_shapes=[
            pltpu.VMEM((N_PAIRS, LANES, LANES), F32),
            pltpu.VMEM((T, D), F32),
        ],
        compiler_params=pltpu.CompilerParams(
            dimension_semantics=("arbitrary",), vmem_limit_bytes=VMEM_LIMIT),
        name="rwkv7_mix",
    )(*args)


def _attn_kernel(slopes_ref, q_ref, k_ref, vt_ref, lq1_ref, lk1_ref, lq2_ref, lk2_ref, sub_ref,
                 o_ref, acc_sc, s_sc):
    tq = ATT_TQ
    tk = ATT_TK
    qb = ATT_QB
    nb = tq // qb
    kv_per_q = tq // tk
    h = pl.program_id(0)
    i = pl.program_id(1)
    slope = slopes_ref[h]
    q_base = ((i * tq) // POS_BLOCK) * POS_BLOCK

    acc_sc[...] = jnp.zeros(acc_sc.shape, F32)
    chains = [(c, b) for c in range(2) for b in range(nb)]
    n = len(chains)
    ahead = ATT_LOOKAHEAD

    def scores(j, ci, rows):
        c, b = chains[ci]
        k0 = pl.multiple_of(j * tk, tk)
        kc = k_ref[pl.ds(k0, rows), c * QK_PAD:(c + 1) * QK_PAD]
        qc = q_ref[b * qb:(b + 1) * qb, c * QK_PAD:(c + 1) * QK_PAD]
        return _dot_nt(kc, qc)

    def accumulate(j, ci, s, m_old, kv_offset):
        c, b = chains[ci]
        rows = s.shape[0]
        k_base = ((j * tk) // POS_BLOCK) * POS_BLOCK
        shift = slope * (q_base - k_base).astype(F32)
        if kv_offset is not None:
            kv_i = lax.broadcasted_iota(jnp.int32, (rows, qb), 0) + kv_offset
            q_i = lax.broadcasted_iota(jnp.int32, (rows, qb), 1) + b * qb
            s = jnp.where(kv_i <= q_i, s, NEG_INF)
        m_new = jnp.maximum(m_old, jnp.max(s, axis=0, keepdims=True) - shift)
        p = jnp.exp(s - (m_new + shift)).astype(BF16)
        alpha = jnp.exp(m_old - m_new)
        cols = slice(b * qb, (b + 1) * qb)
        acc_sc[c, :, cols] = alpha * acc_sc[c, :, cols] + _dot(vt_ref[j, :, 0:rows], p)
        return m_new

    def run(tasks, ms, next_tile):
        ms = list(ms)
        early = {}
        for t, (j, ci, rows, kv_offset) in enumerate(tasks):
            nxt = t + ahead
            if nxt < len(tasks):
                early[nxt] = scores(tasks[nxt][0], tasks[nxt][1], tasks[nxt][2])
            elif next_tile is not None:
                s_sc[nxt - len(tasks)] = scores(next_tile, nxt - len(tasks), tk)
            s = s_sc[t, 0:rows] if t < ahead else early.pop(t)
            ms[ci] = accumulate(j, ci, s, ms[ci], kv_offset)
        return tuple(ms)

    for t in range(ahead):
        s_sc[t] = scores(0, t, tk)
    ms0 = tuple(jnp.full((1, qb), NEG_INF, F32) for _ in chains)
    first_diag = i * kv_per_q
    ms = lax.fori_loop(
        0, i,
        lambda g, ms: run([(g * kv_per_q + d, ci, tk, None) for d in range(kv_per_q) for ci in range(n)],
                          ms, (g + 1) * kv_per_q), ms0)

    tasks = []
    for d in range(kv_per_q):
        for ci, (c, b) in enumerate(chains):
            rows = min(tk, (b + 1) * qb - d * tk)
            if rows > 0:
                crosses = d * tk + rows - 1 > b * qb
                tasks.append((first_diag + d, ci, rows, d * tk if crosses else None))
    run(tasks, ms, None)

    d2 = 2 * DIFF_HEAD_DIM
    lam = (jnp.exp(jnp.sum(lq1_ref[...] * lk1_ref[...], axis=-1, keepdims=True))
           - jnp.exp(jnp.sum(lq2_ref[...] * lk2_ref[...], axis=-1, keepdims=True)) + LAM_INIT)
    a0 = acc_sc[0]
    a1 = acc_sc[1]
    o = a0[0:d2] / a0[d2:d2 + 1] - lam * (a1[0:d2] / a1[d2:d2 + 1])
    ms_o = jnp.mean(o * o, axis=0, keepdims=True)
    o = o * lax.rsqrt(ms_o + SUBLN_EPS) * sub_ref[...] * (1.0 - LAM_INIT)
    o_ref[...] = o.T.astype(o_ref.dtype)


def _diff_attention(qp, kp, vt, lq1, lk1, lq2, lk2, subln):
    S = qp.shape[0]
    tq = ATT_TQ
    d2 = 2 * DIFF_HEAD_DIM
    slopes = jnp.array(_alibi_slopes(), F32)
    vec = lambda z: z.reshape(1, -1).astype(F32)
    resident = functools.partial(pl.BlockSpec, pipeline_mode=pl.Buffered(1))
    return pl.pallas_call(
        _attn_kernel,
        grid=(DIFF_HEADS, S // tq),
        in_specs=[
            pl.BlockSpec(memory_space=pltpu.SMEM),
            pl.BlockSpec((tq, 2 * QK_PAD), lambda h, i: (i, h)),
            resident((S, 2 * QK_PAD), lambda h, i: (0, h)),
            resident((None, S // ATT_TK, VT_ROWS, ATT_TK), lambda h, i: (h, 0, 0, 0)),
            _const_spec((1, DIFF_HEAD_DIM)), _const_spec((1, DIFF_HEAD_DIM)),
            _const_spec((1, DIFF_HEAD_DIM)), _const_spec((1, DIFF_HEAD_DIM)),
            _const_spec((d2, 1)),
        ],
        out_specs=pl.BlockSpec((tq, d2), lambda h, i: (i, h)),
        out_shape=jax.ShapeDtypeStruct((S, DIFF_DIM), BF16),
        scratch_shapes=[pltpu.VMEM((2, VT_ROWS, tq), F32),
                        pltpu.VMEM((ATT_LOOKAHEAD, ATT_TK, ATT_QB), F32)],
        compiler_params=pltpu.CompilerParams(
            dimension_semantics=("arbitrary", "arbitrary"), vmem_limit_bytes=VMEM_LIMIT),
        name="diff_attention",
    )(slopes, qp, kp, vt, vec(lq1), vec(lk1), vec(lq2), vec(lk2),
      subln.reshape(d2, 1).astype(F32))


def _gelu_tanh(x):
    return 0.5 * x * (1.0 + jnp.tanh(math.sqrt(2.0 / math.pi) * (x + 0.044715 * (x * x * x))))


def _ffn_kernel(x_ref, yr_ref, yo_ref, wo_ref, gpost_ref, gpre_ref, wup_ref, cw_ref, cb_ref,
                wdn_ref, gffn_ref, out_ref, tail_sc, hn_sc, acc_sc):
    tm = FFN_ROWS
    cw = FFN_CHUNK

    @pl.when(pl.program_id(0) == 0)
    def _():
        tail_sc[...] = jnp.zeros(tail_sc.shape, F32)

    def rms(z, g_row):
        ms = jnp.mean(z * z, axis=-1, keepdims=True)
        return z * lax.rsqrt(ms + NORM_EPS) * g_row

    half = RWKV_DIM
    mixed = _dot(yr_ref[...], wo_ref[0:half, :]) + _dot(yo_ref[...], wo_ref[half:2 * half, :])
    h = x_ref[...] + rms(mixed, gpost_ref[...])
    hn_sc[...] = rms(h, gpre_ref[...]).astype(BF16)
    acc_sc[...] = jnp.zeros(acc_sc.shape, F32)

    row8 = lax.broadcasted_iota(jnp.int32, (8, cw), 0)

    def up(col0):
        return _dot(hn_sc[...], wup_ref[:, col0:col0 + cw])

    def conv(u, col0):
        tail = tail_sc[:, col0:col0 + cw]
        tail_sc[:, col0:col0 + cw] = u[tm - 8:tm]
        u1 = pltpu.roll(u, 1, axis=0)
        u2 = pltpu.roll(u, 2, axis=0)
        f1 = jnp.where(row8 < 1, pltpu.roll(tail, 1, axis=0), u1[0:8])
        f2 = jnp.where(row8 < 2, pltpu.roll(tail, 2, axis=0), u2[0:8])
        u1 = jnp.concatenate([f1, u1[8:]], axis=0)
        u2 = jnp.concatenate([f2, u2[8:]], axis=0)
        w = cw_ref[:, col0:col0 + cw]
        return (u2 * w[0:1] + u1 * w[1:2] + u * w[2:3]) + cb_ref[:, col0:col0 + cw]

    n_chunks = D_FF // cw
    nxt = (up(0), up(D_FF))
    for c in range(n_chunks):
        u_gate, u_val = nxt
        if c + 1 < n_chunks:
            nxt = (up((c + 1) * cw), up(D_FF + (c + 1) * cw))
        gate = conv(u_gate, c * cw)
        val = conv(u_val, D_FF + c * cw)
        act = (_gelu_tanh(gate) * val).astype(BF16)
        acc_sc[...] += _dot(act, wdn_ref[c * cw:(c + 1) * cw, :])

    out_ref[...] = h + rms(acc_sc[...], gffn_ref[...])


def _out_and_ffn(x2, y_rwkv, y_attn, w_out, g_post, g_pre, w_up, conv_w, conv_b, w_down, g_ffn):
    S = x2.shape[0]
    tm = FFN_ROWS
    row = lambda z: z.reshape(1, -1).astype(F32)
    single = functools.partial(pl.BlockSpec, pipeline_mode=pl.Buffered(1))
    const1 = lambda shape: single(shape, lambda i: (0,) * len(shape))
    return pl.pallas_call(
        _ffn_kernel,
        grid=(S // tm,),
        in_specs=[
            pl.BlockSpec((tm, D_MODEL), lambda i: (i, 0)),
            pl.BlockSpec((tm, RWKV_DIM), lambda i: (i, 0)),
            pl.BlockSpec((tm, DIFF_DIM), lambda i: (i, 0)),
            const1((D_MODEL, D_MODEL)),
            _const_spec((1, D_MODEL)), _const_spec((1, D_MODEL)),
            const1((D_MODEL, 2 * D_FF)),
            _const_spec((CONV_WIDTH, 2 * D_FF)), _const_spec((1, 2 * D_FF)),
            const1((D_FF, D_MODEL)),
            _const_spec((1, D_MODEL)),
        ],
        out_specs=pl.BlockSpec((tm, D_MODEL), lambda i: (i, 0)),
        out_shape=jax.ShapeDtypeStruct((S, D_MODEL), F32),
        scratch_shapes=[
            pltpu.VMEM((8, 2 * D_FF), F32),
            pltpu.VMEM((tm, D_MODEL), BF16),
            pltpu.VMEM((tm, D_MODEL), F32),
        ],
        compiler_params=pltpu.CompilerParams(
            dimension_semantics=("arbitrary",), vmem_limit_bytes=VMEM_LIMIT),
        name="out_ffn",
    )(x2, y_rwkv, y_attn, w_out.astype(BF16), row(g_post), row(g_pre), w_up.astype(BF16),
      conv_w.astype(F32), row(conv_b), w_down.astype(BF16), row(g_ffn))


def kernel(x, ln_attn_pre, w_in, mu_shift, w_decay_up, w_decay0, w_iclr_up, w_iclr0, w_gate_up,
           k_k, k_a, r_k, ln_x_w, ln_x_b, lambda_q1, lambda_k1, lambda_q2, lambda_k2, diff_subln,
           w_out, ln_attn_post, ln_ffn_pre, w_up, conv_w, conv_b, w_down, ln_ffn_post):
    B, S, _ = x.shape
    assert B == 1 and w_in.shape[0] == 1, "single batch, depth 1"
    assert S % POS_BLOCK == 0
    x2 = x[0]
    xs, qp, kp, vt = _in_projection(x2, ln_attn_pre[0], mu_shift[0], w_in[0])
    y_rwkv = _rwkv_mix(xs, w_decay_up[0], w_decay0[0], w_iclr_up[0], w_iclr0[0],
                       w_gate_up[0], k_k[0], k_a[0], r_k[0], ln_x_w[0], ln_x_b[0])
    y_attn = _diff_attention(qp, kp, vt, lambda_q1[0], lambda_k1[0], lambda_q2[0], lambda_k2[0],
                             diff_subln[0])
    out = _out_and_ffn(x2, y_rwkv, y_attn, w_out[0], ln_attn_post[0], ln_ffn_pre[0], w_up[0],
                       conv_w[0], conv_b[0], w_down[0], ln_ffn_post[0])
    return out[None]
```

```python
import functools
import math

import jax
import jax.numpy as jnp
import numpy as np
from jax import lax
from jax.experimental import pallas as pl
from jax.experimental.pallas import tpu as pltpu

F32 = jnp.float32
BF16 = jnp.bfloat16

D_MODEL = 1024
RWKV_HEADS = 8
RWKV_HEAD_DIM = 64
RWKV_DIM = RWKV_HEADS * RWKV_HEAD_DIM
LORA_W = 64
LORA_A = 64
LORA_G = 128
DIFF_HEADS = 4
DIFF_HEAD_DIM = 64
DIFF_DIM = DIFF_HEADS * 2 * DIFF_HEAD_DIM
RWKV_COLS = 3 * RWKV_DIM + LORA_W + LORA_A + LORA_G
D_FF = 2816
CONV_WIDTH = 3
DECAY_SCALE = math.exp(-0.5)
RWKV_LN_EPS = 64e-5
NORM_EPS = 1e-6
SUBLN_EPS = 1e-5
NEG_INF = -1e30
LAM_INIT = 0.8 - 0.6 * math.exp(-0.3 * 0)

LANES = 128
VMEM_LIMIT = 56 * 1024 * 1024

ATT_TQ = 1024
ATT_TK = 512
ATT_QB = 256
ATT_LOOKAHEAD = 3
POS_BLOCK = 1024
POS_SPLIT = 256
N_MAPS = 2 * DIFF_HEADS
QK_PAD = 128
VT_ROWS = 144

CHUNK = 64
RWKV_ROWS = 256
N_PAIRS = RWKV_HEADS // 2

PROJ_ROWS = 512
FFN_ROWS = 512
FFN_CHUNK = 256


def _dot(a, b):
    return jnp.dot(a, b, preferred_element_type=F32)


def _dot_nt(a, b):
    return lax.dot_general(a, b, (((1,), (1,)), ((), ())), preferred_element_type=F32)


def _dot_tn(a, b):
    return lax.dot_general(a, b, (((0,), (0,)), ((), ())), preferred_element_type=F32)


def _const_spec(shape):
    nd = len(shape)
    return pl.BlockSpec(shape, lambda *_: (0,) * nd)


def _inproj_kernel(x_ref, g_ref, mu_ref, w_ref, wvt_ref, qpat_ref, kpat_ref, vtpat_ref,
                   rw_ref, q_ref, k_ref, vt_ref, prev_sc):
    tm = PROJ_ROWS

    @pl.when(pl.program_id(0) == 0)
    def _():
        prev_sc[...] = jnp.zeros(prev_sc.shape, F32)

    x = x_ref[...]
    ms = jnp.mean(x * x, axis=-1, keepdims=True)
    xn = (x * lax.rsqrt(ms + NORM_EPS) * g_ref[...]).astype(BF16)
    c0 = RWKV_COLS
    c1 = c0 + N_MAPS * QK_PAD
    c2 = c1 + N_MAPS * QK_PAD
    p = _dot(xn, w_ref[:, 0:c0])
    row = lax.broadcasted_iota(jnp.int32, (8, RWKV_COLS), 0)
    shifted = pltpu.roll(p, 1, axis=0)
    first = jnp.where(row == 0, prev_sc[...], shifted[0:8])
    shifted = jnp.concatenate([first, shifted[8:]], axis=0)
    prev_sc[...] = jnp.broadcast_to(p[tm - 1:tm, :], prev_sc.shape)
    rw_ref[...] = p + (shifted - p) * mu_ref[...]
    q_ref[...] = (_dot(xn, w_ref[:, c0:c1]) + qpat_ref[...]).astype(BF16)
    k_ref[...] = (_dot(xn, w_ref[:, c1:c2]) + kpat_ref[...]).astype(BF16)
    vt = (_dot_nt(wvt_ref[...], xn) + vtpat_ref[...]).astype(BF16)
    for h in range(DIFF_HEADS):
        vt_ref[h, 0] = vt[h * VT_ROWS:(h + 1) * VT_ROWS]


def _alibi_slopes():
    return [2.0 ** (-8.0 * (i + 1) / DIFF_HEADS) for i in range(DIFF_HEADS)]


def _alibi_patterns():
    pos = np.arange(POS_BLOCK)
    lo = (pos % POS_SPLIT).astype(np.float32)
    hi = (pos - pos % POS_SPLIT).astype(np.float32)
    qpat = np.zeros((POS_BLOCK, N_MAPS, QK_PAD), np.float32)
    kpat = np.zeros((POS_BLOCK, N_MAPS, QK_PAD), np.float32)
    d = DIFF_HEAD_DIM
    for m in range(N_MAPS):
        slope = _alibi_slopes()[m // 2]
        qpat[:, m, d + 0] = -slope * lo
        qpat[:, m, d + 1] = -slope * hi
        qpat[:, m, d + 2:d + 4] = 1.0
        kpat[:, m, d + 0:d + 2] = 1.0
        kpat[:, m, d + 2] = slope * lo
        kpat[:, m, d + 3] = slope * hi
    vtpat = np.zeros((DIFF_HEADS, VT_ROWS, PROJ_ROWS), np.float32)
    vtpat[:, 2 * d, :] = 1.0
    return (jnp.asarray(qpat.reshape(POS_BLOCK, N_MAPS * QK_PAD), BF16),
            jnp.asarray(kpat.reshape(POS_BLOCK, N_MAPS * QK_PAD), BF16),
            jnp.asarray(vtpat.reshape(DIFF_HEADS * VT_ROWS, PROJ_ROWS), BF16))


def _pack_w_in(w_in):
    d = DIFF_HEAD_DIM
    w_rw = w_in[:, :RWKV_COLS]
    wq = w_in[:, RWKV_COLS:RWKV_COLS + DIFF_DIM].reshape(D_MODEL, N_MAPS, d) * (d ** -0.5)
    wk = w_in[:, RWKV_COLS + DIFF_DIM:RWKV_COLS + 2 * DIFF_DIM].reshape(D_MODEL, N_MAPS, d)
    wv = w_in[:, RWKV_COLS + 2 * DIFF_DIM:].reshape(D_MODEL, DIFF_HEADS, 2 * d)
    wq = jnp.pad(wq, ((0, 0), (0, 0), (0, QK_PAD - d))).reshape(D_MODEL, N_MAPS * QK_PAD)
    wk = jnp.pad(wk, ((0, 0), (0, 0), (0, QK_PAD - d))).reshape(D_MODEL, N_MAPS * QK_PAD)
    wvt = jnp.pad(wv.transpose(1, 2, 0), ((0, 0), (0, VT_ROWS - 2 * d), (0, 0)))
    return (jnp.concatenate([w_rw, wq, wk], axis=1).astype(BF16),
            wvt.reshape(DIFF_HEADS * VT_ROWS, D_MODEL).astype(BF16))


def _in_projection(x2, g, mu, w_in):
    S = x2.shape[0]
    tm = PROJ_ROWS
    assert tm == ATT_TK, "the transposed value blocks are consumed one per attention kv step"
    w, wvt = _pack_w_in(w_in)
    qpat, kpat, vtpat = _alibi_patterns()
    nq = N_MAPS * QK_PAD
    rep = POS_BLOCK // tm
    single = functools.partial(pl.BlockSpec, pipeline_mode=pl.Buffered(1))
    return pl.pallas_call(
        _inproj_kernel,
        grid=(S // tm,),
        in_specs=[
            pl.BlockSpec((tm, D_MODEL), lambda i: (i, 0)),
            _const_spec((1, D_MODEL)),
            _const_spec((1, RWKV_COLS)),
            single(w.shape, lambda i: (0, 0)),
            single(wvt.shape, lambda i: (0, 0)),
            pl.BlockSpec((tm, nq), lambda i: (i % rep, 0)),
            pl.BlockSpec((tm, nq), lambda i: (i % rep, 0)),
            _const_spec(vtpat.shape),
        ],
        out_specs=[
            pl.BlockSpec((tm, RWKV_COLS), lambda i: (i, 0)),
            pl.BlockSpec((tm, nq), lambda i: (i, 0)),
            pl.BlockSpec((tm, nq), lambda i: (i, 0)),
            pl.BlockSpec((DIFF_HEADS, 1, VT_ROWS, tm), lambda i: (0, i, 0, 0)),
        ],
        out_shape=[
            jax.ShapeDtypeStruct((S, RWKV_COLS), F32),
            jax.ShapeDtypeStruct((S, nq), BF16),
            jax.ShapeDtypeStruct((S, nq), BF16),
            jax.ShapeDtypeStruct((DIFF_HEADS, S // tm, VT_ROWS, tm), BF16),
        ],
        scratch_shapes=[pltpu.VMEM((8, RWKV_COLS), F32)],
        compiler_params=pltpu.CompilerParams(
            dimension_semantics=("arbitrary",), vmem_limit_bytes=VMEM_LIMIT),
        name="in_projection",
    )(x2, g.reshape(1, D_MODEL), mu.reshape(1, RWKV_COLS).astype(F32), w, wvt, qpat, kpat, vtpat)


def _rwkv_kernel(xs_ref, wwa_ref, wd0_ref, wa0_ref, wg_ref, kk_ref, ka_ref, rk_ref,
                 lnw_ref, lnb_ref, hsum_ref, cum_ref, y_ref, state_sc, ych_sc):
    T = RWKV_ROWS
    C = CHUNK
    D = RWKV_DIM

    @pl.when(pl.program_id(0) == 0)
    def _():
        state_sc[...] = jnp.zeros(state_sc.shape, F32)

    r = xs_ref[:, 0:D]
    k = xs_ref[:, D:2 * D]
    v = xs_ref[:, 2 * D:3 * D]
    wa = xs_ref[:, 3 * D:3 * D + LANES]
    g_lo = xs_ref[:, 3 * D + LANES:3 * D + 2 * LANES]

    lane = lax.broadcasted_iota(jnp.int32, (T, LANES), 1)
    wa_act = jnp.where(lane < LORA_W, jnp.tanh(wa), wa).astype(BF16)
    pre = _dot(wa_act, wwa_ref[...])
    lw = -DECAY_SCALE * jax.nn.sigmoid(wd0_ref[...] + pre[:, 0:D])
    a = jax.nn.sigmoid(wa0_ref[...] + pre[:, D:2 * D])
    g = _dot(jax.nn.sigmoid(g_lo).astype(BF16), wg_ref[...])

    def head_sum(z):
        zb = z.astype(BF16)
        w = hsum_ref.shape[0]
        return jnp.concatenate([_dot(zb[:, o:o + w], hsum_ref[...]) for o in range(0, D, w)], axis=1)

    kk = k * kk_ref[...]
    n2 = head_sum(kk * kk)
    kk = kk * jnp.minimum(lax.rsqrt(n2), 1e12)
    k2 = k * (1.0 + (a - 1.0) * ka_ref[...])
    bonus = head_sum(r * k2 * rk_ref[...])

    lw_hi = lw.astype(BF16)
    lw_lo = (lw - lw_hi.astype(F32)).astype(BF16)
    cum = _dot(cum_ref[...], lw_hi) + _dot(cum_ref[...], lw_lo)
    L = cum[0:T]
    LC = cum[T:2 * T]
    e_fwd = jnp.exp(L)
    e_prev = jnp.exp(L - lw)
    e_inv = jnp.exp(-L)
    e_end = jnp.exp(LC - L)
    p_end = jnp.exp(LC)
    kka = kk * a
    r_t = (r * e_fwd).astype(BF16)
    al_t = (-kk * e_prev)
    k_t = (k2 * e_inv).astype(BF16)
    be_t = (kka * e_inv).astype(BF16)
    k_e = (k2 * e_end).astype(BF16)
    be_e = (kka * e_end).astype(BF16)
    v_b = v.astype(BF16)

    rr = lax.broadcasted_iota(jnp.int32, (2 * C, LANES), 0)
    ll = lax.broadcasted_iota(jnp.int32, (2 * C, LANES), 1)
    bd_mask = (rr < C) == (ll < C)
    tri_r = lax.broadcasted_iota(jnp.int32, (C, LANES), 0)
    tri_l = lax.broadcasted_iota(jnp.int32, (C, LANES), 1) & (C - 1)
    strict = tri_l < tri_r
    incl = tri_l <= tri_r

    def bd(z):
        zb = z.astype(BF16)
        return jnp.where(bd_mask, jnp.concatenate([zb, zb], axis=0), jnp.zeros((), BF16))

    n_chunks = T // C
    pairs = range(N_PAIRS)
    blocks = [(c, pr) for c in range(n_chunks) for pr in pairs]
    blk = {(c, pr): (slice(c * C, (c + 1) * C), slice(pr * LANES, (pr + 1) * LANES))
           for c, pr in blocks}

    gram = {}
    for key in blocks:
        sl = blk[key]
        xs_c = jnp.concatenate([al_t[sl].astype(BF16), r_t[sl]], axis=0)
        y_bd = jnp.concatenate([bd(be_t[sl]), bd(k_t[sl])], axis=0)
        gram[key] = _dot_nt(xs_c, y_bd)
    a_pow = {key: jnp.where(strict, gram[key][0:C, 0:LANES], 0.0) for key in blocks}
    a_r = {key: jnp.concatenate([jnp.where(incl, gram[key][C:2 * C, 0:LANES], 0.0),
                                 jnp.where(incl, gram[key][C:2 * C, LANES:2 * LANES], 0.0)],
                                axis=1).astype(BF16) for key in blocks}
    v_bd = {key: bd(v_b[blk[key]]) for key in blocks}

    x_u = {key: al_t[blk[key]] for key in blocks}
    x_0 = {key: _dot(jnp.where(strict, gram[key][0:C, LANES:2 * LANES], 0.0).astype(BF16), v_bd[key])
           for key in blocks}
    n_steps = int(math.log2(C))
    for step in range(n_steps):
        a_pow_b = {key: a_pow[key].astype(BF16) for key in blocks}
        for key in blocks:
            rhs = jnp.concatenate([bd(x_u[key]), bd(x_0[key])], axis=1)
            upd = _dot(a_pow_b[key], rhs)
            x_u[key] = x_u[key] + upd[:, 0:LANES]
            x_0[key] = x_0[key] + upd[:, LANES:2 * LANES]
        if step + 1 < n_steps:
            a_pow = {key: _dot(a_pow_b[key], bd(a_pow[key])) for key in blocks}

    w_mat = {}
    n_mat = {}
    for key in blocks:
        sl = blk[key]
        w_mat[key] = jnp.where(bd_mask, _dot_tn(x_u[key].astype(BF16), be_e[sl]), 0.0).astype(BF16)
        vu0 = jnp.concatenate([v_b[sl], x_0[key].astype(BF16)], axis=0)
        kb = jnp.concatenate([k_e[sl], be_e[sl]], axis=0)
        n_mat[key] = jnp.where(bd_mask, _dot_tn(vu0, kb), 0.0)

    def emit_outputs(c, u, y_s):
        for pr in pairs:
            ych_sc[blk[c, pr]] = y_s[pr] + _dot(
                a_r[c, pr], jnp.concatenate([bd(u[pr]), v_bd[c, pr]], axis=0))

    st = {pr: state_sc[pr] for pr in pairs}
    pending = None
    for c in range(n_chunks):
        st_b = {pr: st[pr].astype(BF16) for pr in pairs}
        st = {pr: st[pr] * p_end[c * C:c * C + 1, blk[c, pr][1]] + _dot(st_b[pr], w_mat[c, pr])
              + n_mat[c, pr] for pr in pairs}
        if pending is not None:
            emit_outputs(*pending)
        u = {pr: _dot_nt(x_u[c, pr].astype(BF16), st_b[pr]) + x_0[c, pr] for pr in pairs}
        y_s = {pr: _dot_nt(r_t[blk[c, pr]], st_b[pr]) for pr in pairs}
        pending = (c, u, y_s)
    emit_outputs(*pending)
    for pr in pairs:
        state_sc[pr] = st[pr]

    y = ych_sc[...] + bonus * v
    inv_n = 1.0 / RWKV_HEAD_DIM
    mean = head_sum(y) * inv_n
    yc = y - mean
    var = head_sum(yc * yc) * inv_n
    yn = yc * lax.rsqrt(var + RWKV_LN_EPS)
    y_ref[...] = ((yn * lnw_ref[...] + lnb_ref[...]) * g).astype(y_ref.dtype)


def _rwkv_mix(xs, w_decay_up, w_decay0, w_iclr_up, w_iclr0, w_gate_up, k_k, k_a, r_k,
              ln_x_w, ln_x_b):
    S = xs.shape[0]
    T = RWKV_ROWS
    D = RWKV_DIM
    wwa = jnp.zeros((LANES, 2 * D), F32)
    wwa = wwa.at[0:LORA_W, 0:D].set(w_decay_up).at[LORA_W:LORA_W + LORA_A, D:2 * D].set(w_iclr_up)
    hid = np.arange(2 * LANES) // RWKV_HEAD_DIM
    hsum = jnp.asarray(hid[:, None] == hid[None, :], BF16)
    t = np.arange(T)
    same = (t[:, None] // CHUNK) == (t[None, :] // CHUNK)
    cum = jnp.asarray(np.concatenate([same & (t[None, :] <= t[:, None]), same], axis=0), BF16)
    row = lambda z: z.reshape(1, -1).astype(F32)
    args = (xs, wwa.astype(BF16), row(w_decay0), row(w_iclr0), w_gate_up.astype(BF16),
            row(k_k), row(k_a), row(r_k), row(ln_x_w), row(ln_x_b), hsum, cum)
    in_specs = [pl.BlockSpec((T, RWKV_COLS), lambda i: (i, 0))]
    in_specs += [_const_spec(z.shape) for z in args[1:]]
    return pl.pallas_call(
        _rwkv_kernel,
        grid=(S // T,),
        in_specs=in_specs,
        out_specs=pl.BlockSpec((T, D), lambda i: (i, 0)),
        out_shape=jax.ShapeDtypeStruct((S, D), BF16),
        scratch_shapes=[
            pltpu.VMEM((N_PAIRS, LANES, LANES), F32),
            pltpu.VMEM((T, D), F32),
        ],
        compiler_params=pltpu.CompilerParams(
            dimension_semantics=("arbitrary",), vmem_limit_bytes=VMEM_LIMIT),
        name="rwkv7_mix",
    )(*args)


def _attn_kernel(slopes_ref, q_ref, k_ref, vt_ref, lq1_ref, lk1_ref, lq2_ref, lk2_ref, sub_ref,
                 o_ref, acc_sc, s_sc):
    tq = ATT_TQ
    tk = ATT_TK
    qb = ATT_QB
    nb = tq // qb
    kv_per_q = tq // tk
    h = pl.program_id(0)
    i = pl.program_id(1)
    slope = slopes_ref[h]
    q_base = ((i * tq) // POS_BLOCK) * POS_BLOCK

    acc_sc[...] = jnp.zeros(acc_sc.shape, F32)
    chains = [(c, b) for c in range(2) for b in range(nb)]
    qs = [q_ref[b * qb:(b + 1) * qb, c * QK_PAD:(c + 1) * QK_PAD] for c, b in chains]
    n = len(chains)
    ahead = ATT_LOOKAHEAD

    def scores(j, ci, rows):
        c, _ = chains[ci]
        k0 = pl.multiple_of(j * tk, tk)
        kc = k_ref[pl.ds(k0, rows), c * QK_PAD:(c + 1) * QK_PAD]
        return _dot_nt(kc, qs[ci])

    def accumulate(j, ci, s, m_old, kv_offset):
        c, b = chains[ci]
        rows = s.shape[0]
        k_base = ((j * tk) // POS_BLOCK) * POS_BLOCK
        shift = slope * (q_base - k_base).astype(F32)
        if kv_offset is not None:
            kv_i = lax.broadcasted_iota(jnp.int32, (rows, qb), 0) + kv_offset
            q_i = lax.broadcasted_iota(jnp.int32, (rows, qb), 1) + b * qb
            s = jnp.where(kv_i <= q_i, s, NEG_INF)
        m_new = jnp.maximum(m_old, jnp.max(s, axis=0, keepdims=True) - shift)
        p = jnp.exp(s - (m_new + shift)).astype(BF16)
        alpha = jnp.exp(m_old - m_new)
        cols = slice(b * qb, (b + 1) * qb)
        acc_sc[c, :, cols] = alpha * acc_sc[c, :, cols] + _dot(vt_ref[j, :, 0:rows], p)
        return m_new

    def run(tasks, ms, next_tile):
        ms = list(ms)
        early = {}
        for t, (j, ci, rows, kv_offset) in enumerate(tasks):
            nxt = t + ahead
            if nxt < len(tasks):
                early[nxt] = scores(tasks[nxt][0], tasks[nxt][1], tasks[nxt][2])
            elif next_tile is not None:
                s_sc[nxt - len(tasks)] = scores(next_tile, nxt - len(tasks), tk)
            s = s_sc[t, 0:rows] if t < ahead else early.pop(t)
            ms[ci] = accumulate(j, ci, s, ms[ci], kv_offset)
        return tuple(ms)

    for t in range(ahead):
        s_sc[t] = scores(0, t, tk)
    ms0 = tuple(jnp.full((1, qb), NEG_INF, F32) for _ in chains)
    first_diag = i * kv_per_q
    ms = lax.fori_loop(
        0, i,
        lambda g, ms: run([(g * kv_per_q + d, ci, tk, None) for d in range(kv_per_q) for ci in range(n)],
                          ms, (g + 1) * kv_per_q), ms0)

    tasks = []
    for d in range(kv_per_q):
        for ci, (c, b) in enumerate(chains):
            rows = min(tk, (b + 1) * qb - d * tk)
            if rows > 0:
                crosses = d * tk + rows - 1 > b * qb
                tasks.append((first_diag + d, ci, rows, d * tk if crosses else None))
    run(tasks, ms, None)

    d2 = 2 * DIFF_HEAD_DIM
    lam = (jnp.exp(jnp.sum(lq1_ref[...] * lk1_ref[...], axis=-1, keepdims=True))
           - jnp.exp(jnp.sum(lq2_ref[...] * lk2_ref[...], axis=-1, keepdims=True)) + LAM_INIT)
    a0 = acc_sc[0]
    a1 = acc_sc[1]
    o = a0[0:d2] / a0[d2:d2 + 1] - lam * (a1[0:d2] / a1[d2:d2 + 1])
    ms_o = jnp.mean(o * o, axis=0, keepdims=True)
    o = o * lax.rsqrt(ms_o + SUBLN_EPS) * sub_ref[...] * (1.0 - LAM_INIT)
    o_ref[...] = o.T.astype(o_ref.dtype)


def _diff_attention(qp, kp, vt, lq1, lk1, lq2, lk2, subln):
    S = qp.shape[0]
    tq = ATT_TQ
    d2 = 2 * DIFF_HEAD_DIM
    slopes = jnp.array(_alibi_slopes(), F32)
    vec = lambda z: z.reshape(1, -1).astype(F32)
    resident = functools.partial(pl.BlockSpec, pipeline_mode=pl.Buffered(1))
    return pl.pallas_call(
        _attn_kernel,
        grid=(DIFF_HEADS, S // tq),
        in_specs=[
            pl.BlockSpec(memory_space=pltpu.SMEM),
            pl.BlockSpec((tq, 2 * QK_PAD), lambda h, i: (i, h)),
            resident((S, 2 * QK_PAD), lambda h, i: (0, h)),
            resident((None, S // ATT_TK, VT_ROWS, ATT_TK), lambda h, i: (h, 0, 0, 0)),
            _const_spec((1, DIFF_HEAD_DIM)), _const_spec((1, DIFF_HEAD_DIM)),
            _const_spec((1, DIFF_HEAD_DIM)), _const_spec((1, DIFF_HEAD_DIM)),
            _const_spec((d2, 1)),
        ],
        out_specs=pl.BlockSpec((tq, d2), lambda h, i: (i, h)),
        out_shape=jax.ShapeDtypeStruct((S, DIFF_DIM), BF16),
        scratch_shapes=[pltpu.VMEM((2, VT_ROWS, tq), F32),
                        pltpu.VMEM((ATT_LOOKAHEAD, ATT_TK, ATT_QB), F32)],
        compiler_params=pltpu.CompilerParams(
            dimension_semantics=("arbitrary", "arbitrary"), vmem_limit_bytes=VMEM_LIMIT),
        name="diff_attention",
    )(slopes, qp, kp, vt, vec(lq1), vec(lk1), vec(lq2), vec(lk2),
      subln.reshape(d2, 1).astype(F32))


def _gelu_tanh(x):
    return 0.5 * x * (1.0 + jnp.tanh(math.sqrt(2.0 / math.pi) * (x + 0.044715 * (x * x * x))))


def _ffn_kernel(x_ref, yr_ref, yo_ref, wo_ref, gpost_ref, gpre_ref, wup_ref, cw_ref, cb_ref,
                wdn_ref, gffn_ref, out_ref, tail_sc, hn_sc, acc_sc):
    tm = FFN_ROWS
    cw = FFN_CHUNK

    @pl.when(pl.program_id(0) == 0)
    def _():
        tail_sc[...] = jnp.zeros(tail_sc.shape, F32)

    def rms(z, g_row):
        ms = jnp.mean(z * z, axis=-1, keepdims=True)
        return z * lax.rsqrt(ms + NORM_EPS) * g_row

    half = RWKV_DIM
    mixed = _dot(yr_ref[...], wo_ref[0:half, :]) + _dot(yo_ref[...], wo_ref[half:2 * half, :])
    h = x_ref[...] + rms(mixed, gpost_ref[...])
    hn_sc[...] = rms(h, gpre_ref[...]).astype(BF16)
    acc_sc[...] = jnp.zeros(acc_sc.shape, F32)

    sub8 = lax.broadcasted_iota(jnp.int32, (1, 8, cw), 1)

    def up(col0):
        return _dot(hn_sc[...], wup_ref[:, col0:col0 + cw])

    def conv(u, col0):
        tail = tail_sc[:, col0:col0 + cw]
        tail_sc[:, col0:col0 + cw] = u[tm - 8:tm]
        def shifted(k):
            rot = pltpu.roll(jnp.concatenate([tail, u], axis=0).reshape(tm // 8 + 1, 8, cw), k, axis=1)
            return jnp.where(sub8 < k, rot[:-1], rot[1:]).reshape(tm, cw)

        w = cw_ref[:, col0:col0 + cw]
        return (shifted(2) * w[0:1] + shifted(1) * w[1:2] + u * w[2:3]) + cb_ref[:, col0:col0 + cw]

    n_chunks = D_FF // cw
    nxt = (up(0), up(D_FF))
    for c in range(n_chunks):
        u_gate, u_val = nxt
        if c + 1 < n_chunks:
            nxt = (up((c + 1) * cw), up(D_FF + (c + 1) * cw))
        gate = conv(u_gate, c * cw)
        val = conv(u_val, D_FF + c * cw)
        act = (_gelu_tanh(gate) * val).astype(BF16)
        acc_sc[...] += _dot(act, wdn_ref[c * cw:(c + 1) * cw, :])

    out_ref[...] = h + rms(acc_sc[...], gffn_ref[...])


def _out_and_ffn(x2, y_rwkv, y_attn, w_out, g_post, g_pre, w_up, conv_w, conv_b, w_down, g_ffn):
    S = x2.shape[0]
    tm = FFN_ROWS
    row = lambda z: z.reshape(1, -1).astype(F32)
    single = functools.partial(pl.BlockSpec, pipeline_mode=pl.Buffered(1))
    const1 = lambda shape: single(shape, lambda i: (0,) * len(shape))
    return pl.pallas_call(
        _ffn_kernel,
        grid=(S // tm,),
        in_specs=[
            pl.BlockSpec((tm, D_MODEL), lambda i: (i, 0)),
            pl.BlockSpec((tm, RWKV_DIM), lambda i: (i, 0)),
            pl.BlockSpec((tm, DIFF_DIM), lambda i: (i, 0)),
            const1((D_MODEL, D_MODEL)),
            _const_spec((1, D_MODEL)), _const_spec((1, D_MODEL)),
            const1((D_MODEL, 2 * D_FF)),
            _const_spec((CONV_WIDTH, 2 * D_FF)), _const_spec((1, 2 * D_FF)),
            const1((D_FF, D_MODEL)),
            _const_spec((1, D_MODEL)),
        ],
        out_specs=pl.BlockSpec((tm, D_MODEL), lambda i: (i, 0)),
        out_shape=jax.ShapeDtypeStruct((S, D_MODEL), F32),
        scratch_shapes=[
            pltpu.VMEM((8, 2 * D_FF), F32),
            pltpu.VMEM((tm, D_MODEL), BF16),
            pltpu.VMEM((tm, D_MODEL), F32),
        ],
        compiler_params=pltpu.CompilerParams(
            dimension_semantics=("arbitrary",), vmem_limit_bytes=VMEM_LIMIT),
        name="out_ffn",
    )(x2, y_rwkv, y_attn, w_out.astype(BF16), row(g_post), row(g_pre), w_up.astype(BF16),
      conv_w.astype(F32), row(conv_b), w_down.astype(BF16), row(g_ffn))


def kernel(x, ln_attn_pre, w_in, mu_shift, w_decay_up, w_decay0, w_iclr_up, w_iclr0, w_gate_up,
           k_k, k_a, r_k, ln_x_w, ln_x_b, lambda_q1, lambda_k1, lambda_q2, lambda_k2, diff_subln,
           w_out, ln_attn_post, ln_ffn_pre, w_up, conv_w, conv_b, w_down, ln_ffn_post):
    B, S, _ = x.shape
    assert B == 1 and w_in.shape[0] == 1, "single batch, depth 1"
    assert S % POS_BLOCK == 0
    x2 = x[0]
    xs, qp, kp, vt = _in_projection(x2, ln_attn_pre[0], mu_shift[0], w_in[0])
    y_rwkv = _rwkv_mix(xs, w_decay_up[0], w_decay0[0], w_iclr_up[0], w_iclr0[0],
                       w_gate_up[0], k_k[0], k_a[0], r_k[0], ln_x_w[0], ln_x_b[0])
    y_attn = _diff_attention(qp, kp, vt, lambda_q1[0], lambda_k1[0], lambda_q2[0], lambda_k2[0],
                             diff_subln[0])
    out = _out_and_ffn(x2, y_rwkv, y_attn, w_out[0], ln_attn_post[0], ln_ffn_pre[0], w_up[0],
                       conv_w[0], conv_b[0], w_down[0], ln_ffn_post[0])
    return out[None]
```

```python
import functools
import math

import jax
import jax.numpy as jnp
import numpy as np
from jax import lax
from jax.experimental import pallas as pl
from jax.experimental.pallas import tpu as pltpu

F32 = jnp.float32
BF16 = jnp.bfloat16

D_MODEL = 1024
RWKV_HEADS = 8
RWKV_HEAD_DIM = 64
RWKV_DIM = RWKV_HEADS * RWKV_HEAD_DIM
LORA_W = 64
LORA_A = 64
LORA_G = 128
DIFF_HEADS = 4
DIFF_HEAD_DIM = 64
DIFF_DIM = DIFF_HEADS * 2 * DIFF_HEAD_DIM
RWKV_COLS = 3 * RWKV_DIM + LORA_W + LORA_A + LORA_G
D_FF = 2816
CONV_WIDTH = 3
DECAY_SCALE = math.exp(-0.5)
RWKV_LN_EPS = 64e-5
NORM_EPS = 1e-6
SUBLN_EPS = 1e-5
NEG_INF = -1e30
LAM_INIT = 0.8 - 0.6 * math.exp(-0.3 * 0)

LANES = 128
VMEM_LIMIT = 56 * 1024 * 1024

ATT_TQ = 1024
ATT_TK = 512
ATT_QB = 256
ATT_LOOKAHEAD = 3
POS_BLOCK = 1024
POS_SPLIT = 256
N_MAPS = 2 * DIFF_HEADS
QK_PAD = 128
VT_ROWS = 144

CHUNK = 64
RWKV_ROWS = 256
N_PAIRS = RWKV_HEADS // 2

PROJ_ROWS = 512
FFN_ROWS = 512
FFN_CHUNK = 256


def _dot(a, b):
    return jnp.dot(a, b, preferred_element_type=F32)


def _dot_nt(a, b):
    return lax.dot_general(a, b, (((1,), (1,)), ((), ())), preferred_element_type=F32)


def _dot_tn(a, b):
    return lax.dot_general(a, b, (((0,), (0,)), ((), ())), preferred_element_type=F32)


def _const_spec(shape):
    nd = len(shape)
    return pl.BlockSpec(shape, lambda *_: (0,) * nd)


def _inproj_kernel(x_ref, g_ref, mu_ref, w_ref, wvt_ref, qpat_ref, kpat_ref, vtpat_ref,
                   rw_ref, q_ref, k_ref, vt_ref, prev_sc):
    tm = PROJ_ROWS

    @pl.when(pl.program_id(0) == 0)
    def _():
        prev_sc[...] = jnp.zeros(prev_sc.shape, F32)

    x = x_ref[...]
    ms = jnp.mean(x * x, axis=-1, keepdims=True)
    xn = (x * lax.rsqrt(ms + NORM_EPS) * g_ref[...]).astype(BF16)
    c0 = RWKV_COLS
    c1 = c0 + DIFF_DIM
    c2 = c1 + DIFF_DIM
    p = _dot(xn, w_ref[:, 0:c0])
    row = lax.broadcasted_iota(jnp.int32, (8, RWKV_COLS), 0)
    shifted = pltpu.roll(p, 1, axis=0)
    first = jnp.where(row == 0, prev_sc[...], shifted[0:8])
    shifted = jnp.concatenate([first, shifted[8:]], axis=0)
    prev_sc[...] = jnp.broadcast_to(p[tm - 1:tm, :], prev_sc.shape)
    rw_ref[...] = p + (shifted - p) * mu_ref[...]

    low_half = lax.broadcasted_iota(jnp.int32, (tm, LANES), 1) < DIFF_HEAD_DIM

    def spread(z, pat_ref, out_ref):
        for j in range(DIFF_DIM // LANES):
            two_maps = z[:, j * LANES:(j + 1) * LANES]
            for odd in range(2):
                m = 2 * j + odd
                piece = pltpu.roll(two_maps, DIFF_HEAD_DIM, axis=1) if odd else two_maps
                slot = slice(m * QK_PAD, (m + 1) * QK_PAD)
                out_ref[:, slot] = (jnp.where(low_half, piece, 0.0) + pat_ref[:, slot]).astype(BF16)

    spread(_dot(xn, w_ref[:, c0:c1]), qpat_ref, q_ref)
    spread(_dot(xn, w_ref[:, c1:c2]), kpat_ref, k_ref)
    vt =(_dot_nt(wvt_ref[...], xn) + vtpat_ref[...]).astype(BF16)
    for h in range(DIFF_HEADS):
        vt_ref[h, 0] = vt[h * VT_ROWS:(h + 1) * VT_ROWS]


def _alibi_slopes():
    return [2.0 ** (-8.0 * (i + 1) / DIFF_HEADS) for i in range(DIFF_HEADS)]


def _alibi_patterns():
    pos = np.arange(POS_BLOCK)
    lo = (pos % POS_SPLIT).astype(np.float32)
    hi = (pos - pos % POS_SPLIT).astype(np.float32)
    qpat = np.zeros((POS_BLOCK, N_MAPS, QK_PAD), np.float32)
    kpat = np.zeros((POS_BLOCK, N_MAPS, QK_PAD), np.float32)
    d = DIFF_HEAD_DIM
    for m in range(N_MAPS):
        slope = _alibi_slopes()[m // 2]
        qpat[:, m, d + 0] = -slope * lo
        qpat[:, m, d + 1] = -slope * hi
        qpat[:, m, d + 2:d + 4] = 1.0
        kpat[:, m, d + 0:d + 2] = 1.0
        kpat[:, m, d + 2] = slope * lo
        kpat[:, m, d + 3] = slope * hi
    vtpat = np.zeros((DIFF_HEADS, VT_ROWS, PROJ_ROWS), np.float32)
    vtpat[:, 2 * d, :] = 1.0
    return (jnp.asarray(qpat.reshape(POS_BLOCK, N_MAPS * QK_PAD), BF16),
            jnp.asarray(kpat.reshape(POS_BLOCK, N_MAPS * QK_PAD), BF16),
            jnp.asarray(vtpat.reshape(DIFF_HEADS * VT_ROWS, PROJ_ROWS), BF16))


def _pack_w_in(w_in):
    d = DIFF_HEAD_DIM
    w_rw = w_in[:, :RWKV_COLS]
    wq = w_in[:, RWKV_COLS:RWKV_COLS + DIFF_DIM] * (d ** -0.5)
    wk = w_in[:, RWKV_COLS + DIFF_DIM:RWKV_COLS + 2 * DIFF_DIM]
    wv = w_in[:, RWKV_COLS + 2 * DIFF_DIM:].reshape(D_MODEL, DIFF_HEADS, 2 * d)
    wvt = jnp.pad(wv.transpose(1, 2, 0), ((0, 0), (0, VT_ROWS - 2 * d), (0, 0)))
    return (jnp.concatenate([w_rw, wq, wk], axis=1).astype(BF16),
            wvt.reshape(DIFF_HEADS * VT_ROWS, D_MODEL).astype(BF16))


def _in_projection(x2, g, mu, w_in):
    S = x2.shape[0]
    tm = PROJ_ROWS
    assert tm == ATT_TK, "the transposed value blocks are consumed one per attention kv step"
    w, wvt = _pack_w_in(w_in)
    qpat, kpat, vtpat = _alibi_patterns()
    nq = N_MAPS * QK_PAD
    rep = POS_BLOCK // tm
    single = functools.partial(pl.BlockSpec, pipeline_mode=pl.Buffered(1))
    return pl.pallas_call(
        _inproj_kernel,
        grid=(S // tm,),
        in_specs=[
            pl.BlockSpec((tm, D_MODEL), lambda i: (i, 0)),
            _const_spec((1, D_MODEL)),
            _const_spec((1, RWKV_COLS)),
            single(w.shape, lambda i: (0, 0)),
            single(wvt.shape, lambda i: (0, 0)),
            pl.BlockSpec((tm, nq), lambda i: (i % rep, 0)),
            pl.BlockSpec((tm, nq), lambda i: (i % rep, 0)),
            _const_spec(vtpat.shape),
        ],
        out_specs=[
            pl.BlockSpec((tm, RWKV_COLS), lambda i: (i, 0)),
            pl.BlockSpec((tm, nq), lambda i: (i, 0)),
            pl.BlockSpec((tm, nq), lambda i: (i, 0)),
            pl.BlockSpec((DIFF_HEADS, 1, VT_ROWS, tm), lambda i: (0, i, 0, 0)),
        ],
        out_shape=[
            jax.ShapeDtypeStruct((S, RWKV_COLS), F32),
            jax.ShapeDtypeStruct((S, nq), BF16),
            jax.ShapeDtypeStruct((S, nq), BF16),
            jax.ShapeDtypeStruct((DIFF_HEADS, S // tm, VT_ROWS, tm), BF16),
        ],
        scratch_shapes=[pltpu.VMEM((8, RWKV_COLS), F32)],
        compiler_params=pltpu.CompilerParams(
            dimension_semantics=("arbitrary",), vmem_limit_bytes=VMEM_LIMIT),
        name="in_projection",
    )(x2, g.reshape(1, D_MODEL), mu.reshape(1, RWKV_COLS).astype(F32), w, wvt, qpat, kpat, vtpat)


def _rwkv_kernel(xs_ref, wwa_ref, wd0_ref, wa0_ref, wg_ref, kk_ref, ka_ref, rk_ref,
                 lnw_ref, lnb_ref, hsum_ref, cum_ref, y_ref, state_sc, ych_sc):
    T = RWKV_ROWS
    C = CHUNK
    D = RWKV_DIM

    @pl.when(pl.program_id(0) == 0)
    def _():
        state_sc[...] = jnp.zeros(state_sc.shape, F32)

    r = xs_ref[:, 0:D]
    k = xs_ref[:, D:2 * D]
    v = xs_ref[:, 2 * D:3 * D]
    wa = xs_ref[:, 3 * D:3 * D + LANES]
    g_lo = xs_ref[:, 3 * D + LANES:3 * D + 2 * LANES]

    lane = lax.broadcasted_iota(jnp.int32, (T, LANES), 1)
    wa_act = jnp.where(lane < LORA_W, jnp.tanh(wa), wa).astype(BF16)
    pre = _dot(wa_act, wwa_ref[...])
    lw = -DECAY_SCALE * jax.nn.sigmoid(wd0_ref[...] + pre[:, 0:D])
    a = jax.nn.sigmoid(wa0_ref[...] + pre[:, D:2 * D])
    g = _dot(jax.nn.sigmoid(g_lo).astype(BF16), wg_ref[...])

    def head_sum(z):
        zb = z.astype(BF16)
        w = hsum_ref.shape[0]
        return jnp.concatenate([_dot(zb[:, o:o + w], hsum_ref[...]) for o in range(0, D, w)], axis=1)

    kk = k * kk_ref[...]
    n2 = head_sum(kk * kk)
    kk = kk * jnp.minimum(lax.rsqrt(n2), 1e12)
    k2 = k * (1.0 + (a - 1.0) * ka_ref[...])
    bonus = head_sum(r * k2 * rk_ref[...])

    lw_hi = lw.astype(BF16)
    lw_lo = (lw - lw_hi.astype(F32)).astype(BF16)
    cum = _dot(cum_ref[...], lw_hi) + _dot(cum_ref[...], lw_lo)
    L = cum[0:T]
    LC = cum[T:2 * T]
    e_fwd = jnp.exp(L)
    e_prev = jnp.exp(L - lw)
    e_inv = jnp.exp(-L)
    e_end = jnp.exp(LC - L)
    p_end = jnp.exp(LC)
    kka = kk * a
    r_t = (r * e_fwd).astype(BF16)
    al_t = (-kk * e_prev)
    k_t = (k2 * e_inv).astype(BF16)
    be_t = (kka * e_inv).astype(BF16)
    k_e = (k2 * e_end).astype(BF16)
    be_e = (kka * e_end).astype(BF16)
    v_b = v.astype(BF16)

    rr = lax.broadcasted_iota(jnp.int32, (2 * C, LANES), 0)
    ll = lax.broadcasted_iota(jnp.int32, (2 * C, LANES), 1)
    bd_mask = (rr < C) == (ll < C)
    tri_r = lax.broadcasted_iota(jnp.int32, (C, LANES), 0)
    tri_l = lax.broadcasted_iota(jnp.int32, (C, LANES), 1) & (C - 1)
    strict = tri_l < tri_r
    incl = tri_l <= tri_r

    def bd(z):
        zb = z.astype(BF16)
        return jnp.where(bd_mask, jnp.concatenate([zb, zb], axis=0), jnp.zeros((), BF16))

    n_chunks = T // C
    pairs = range(N_PAIRS)
    blocks = [(c, pr) for c in range(n_chunks) for pr in pairs]
    blk = {(c, pr): (slice(c * C, (c + 1) * C), slice(pr * LANES, (pr + 1) * LANES))
           for c, pr in blocks}

    gram = {}
    for key in blocks:
        sl = blk[key]
        xs_c = jnp.concatenate([al_t[sl].astype(BF16), r_t[sl]], axis=0)
        y_bd = jnp.concatenate([bd(be_t[sl]), bd(k_t[sl])], axis=0)
        gram[key] = _dot_nt(xs_c, y_bd)
    a_pow = {key: jnp.where(strict, gram[key][0:C, 0:LANES], 0.0) for key in blocks}
    a_r = {key: jnp.concatenate([jnp.where(incl, gram[key][C:2 * C, 0:LANES], 0.0),
                                 jnp.where(incl, gram[key][C:2 * C, LANES:2 * LANES], 0.0)],
                                axis=1).astype(BF16) for key in blocks}
    v_bd = {key: bd(v_b[blk[key]]) for key in blocks}

    x_u = {key: al_t[blk[key]] for key in blocks}
    x_0 = {key: _dot(jnp.where(strict, gram[key][0:C, LANES:2 * LANES], 0.0).astype(BF16), v_bd[key])
           for key in blocks}
    n_steps = int(math.log2(C))
    for step in range(n_steps):
        a_pow_b = {key: a_pow[key].astype(BF16) for key in blocks}
        for key in blocks:
            rhs = jnp.concatenate([bd(x_u[key]), bd(x_0[key])], axis=1)
            upd = _dot(a_pow_b[key], rhs)
            x_u[key] = x_u[key] + upd[:, 0:LANES]
            x_0[key] = x_0[key] + upd[:, LANES:2 * LANES]
        if step + 1 < n_steps:
            a_pow = {key: _dot(a_pow_b[key], bd(a_pow[key])) for key in blocks}

    w_mat = {}
    n_mat = {}
    for key in blocks:
        sl = blk[key]
        w_mat[key] = jnp.where(bd_mask, _dot_tn(x_u[key].astype(BF16), be_e[sl]), 0.0).astype(BF16)
        vu0 = jnp.concatenate([v_b[sl], x_0[key].astype(BF16)], axis=0)
        kb = jnp.concatenate([k_e[sl], be_e[sl]], axis=0)
        n_mat[key] = jnp.where(bd_mask, _dot_tn(vu0, kb), 0.0)

    def emit_outputs(c, u, y_s):
        for pr in pairs:
            ych_sc[blk[c, pr]] = y_s[pr] + _dot(
                a_r[c, pr], jnp.concatenate([bd(u[pr]), v_bd[c, pr]], axis=0))

    st = {pr: state_sc[pr] for pr in pairs}
    pending = None
    for c in range(n_chunks):
        st_b = {pr: st[pr].astype(BF16) for pr in pairs}
        st = {pr: st[pr] * p_end[c * C:c * C + 1, blk[c, pr][1]] + _dot(st_b[pr], w_mat[c, pr])
              + n_mat[c, pr] for pr in pairs}
        if pending is not None:
            emit_outputs(*pending)
        u = {pr: _dot_nt(x_u[c, pr].astype(BF16), st_b[pr]) + x_0[c, pr] for pr in pairs}
        y_s = {pr: _dot_nt(r_t[blk[c, pr]], st_b[pr]) for pr in pairs}
        pending = (c, u, y_s)
    emit_outputs(*pending)
    for pr in pairs:
        state_sc[pr] = st[pr]

    y = ych_sc[...] + bonus * v
    inv_n = 1.0 / RWKV_HEAD_DIM
    mean = head_sum(y) * inv_n
    yc = y - mean
    var = head_sum(yc * yc) * inv_n
    yn = yc * lax.rsqrt(var + RWKV_LN_EPS)
    y_ref[...] = ((yn * lnw_ref[...] + lnb_ref[...]) * g).astype(y_ref.dtype)


def _rwkv_mix(xs, w_decay_up, w_decay0, w_iclr_up, w_iclr0, w_gate_up, k_k, k_a, r_k,
              ln_x_w, ln_x_b):
    S = xs.shape[0]
    T = RWKV_ROWS
    D = RWKV_DIM
    wwa = jnp.zeros((LANES, 2 * D), F32)
    wwa = wwa.at[0:LORA_W, 0:D].set(w_decay_up).at[LORA_W:LORA_W + LORA_A, D:2 * D].set(w_iclr_up)
    hid = np.arange(2 * LANES) // RWKV_HEAD_DIM
    hsum = jnp.asarray(hid[:, None] == hid[None, :], BF16)
    t = np.arange(T)
    same = (t[:, None] // CHUNK) == (t[None, :] // CHUNK)
    cum = jnp.asarray(np.concatenate([same & (t[None, :] <= t[:, None]), same], axis=0), BF16)
    row = lambda z: z.reshape(1, -1).astype(F32)
    args = (xs, wwa.astype(BF16), row(w_decay0), row(w_iclr0), w_gate_up.astype(BF16),
            row(k_k), row(k_a), row(r_k), row(ln_x_w), row(ln_x_b), hsum, cum)
    in_specs = [pl.BlockSpec((T, RWKV_COLS), lambda i: (i, 0))]
    in_specs += [_const_spec(z.shape) for z in args[1:]]
    return pl.pallas_call(
        _rwkv_kernel,
        grid=(S // T,),
        in_specs=in_specs,
        out_specs=pl.BlockSpec((T, D), lambda i: (i, 0)),
        out_shape=jax.ShapeDtypeStruct((S, D), BF16),
        scratch_shapes=[
            pltpu.VMEM((N_PAIRS, LANES, LANES), F32),
            pltpu.VMEM((T, D), F32),
        ],
        compiler_params=pltpu.CompilerParams(
            dimension_semantics=("arbitrary",), vmem_limit_bytes=VMEM_LIMIT),
        name="rwkv7_mix",
    )(*args)


def _attn_kernel(slopes_ref, q_ref, k_ref, vt_ref, lq1_ref, lk1_ref, lq2_ref, lk2_ref, sub_ref,
                 o_ref, acc_sc, s_sc):
    tq = ATT_TQ
    tk = ATT_TK
    qb = ATT_QB
    nb = tq // qb
    kv_per_q = tq // tk
    h = pl.program_id(0)
    i = pl.program_id(1)
    slope = slopes_ref[h]
    q_base = ((i * tq) // POS_BLOCK) * POS_BLOCK

    acc_sc[...] = jnp.zeros(acc_sc.shape, F32)
    chains = [(c, b) for c in range(2) for b in range(nb)]
    qs = [q_ref[b * qb:(b + 1) * qb, c * QK_PAD:(c + 1) * QK_PAD] for c, b in chains]
    n = len(chains)
    ahead = ATT_LOOKAHEAD

    def scores(j, ci, rows):
        c, _ = chains[ci]
        k0 = pl.multiple_of(j * tk, tk)
        kc = k_ref[pl.ds(k0, rows), c * QK_PAD:(c + 1) * QK_PAD]
        return _dot_nt(kc, qs[ci])

    def accumulate(j, ci, s, m_old, kv_offset):
        c, b = chains[ci]
        rows = s.shape[0]
        k_base = ((j * tk) // POS_BLOCK) * POS_BLOCK
        shift = slope * (q_base - k_base).astype(F32)
        if kv_offset is not None:
            kv_i = lax.broadcasted_iota(jnp.int32, (rows, qb), 0) + kv_offset
            q_i = lax.broadcasted_iota(jnp.int32, (rows, qb), 1) + b * qb
            s = jnp.where(kv_i <= q_i, s, NEG_INF)
        m_new = jnp.maximum(m_old, jnp.max(s, axis=0, keepdims=True) - shift)
        p = jnp.exp(s - (m_new + shift)).astype(BF16)
        alpha = jnp.exp(m_old - m_new)
        cols = slice(b * qb, (b + 1) * qb)
        acc_sc[c, :, cols] = alpha * acc_sc[c, :, cols] + _dot(vt_ref[j, :, 0:rows], p)
        return m_new

    def run(tasks, ms, next_tile):
        ms = list(ms)
        early = {}
        for t, (j, ci, rows, kv_offset) in enumerate(tasks):
            nxt = t + ahead
            if nxt < len(tasks):
                early[nxt] = scores(tasks[nxt][0], tasks[nxt][1], tasks[nxt][2])
            elif next_tile is not None:
                s_sc[nxt - len(tasks)] = scores(next_tile, nxt - len(tasks), tk)
            s = s_sc[t, 0:rows] if t < ahead else early.pop(t)
            ms[ci] = accumulate(j, ci, s, ms[ci], kv_offset)
        return tuple(ms)

    for t in range(ahead):
        s_sc[t] = scores(0, t, tk)
    ms0 = tuple(jnp.full((1, qb), NEG_INF, F32) for _ in chains)
    first_diag = i * kv_per_q
    ms = lax.fori_loop(
        0, i,
        lambda g, ms: run([(g * kv_per_q + d, ci, tk, None) for d in range(kv_per_q) for ci in range(n)],
                          ms, (g + 1) * kv_per_q), ms0)

    tasks = []
    for d in range(kv_per_q):
        for ci, (c, b) in enumerate(chains):
            rows = min(tk, (b + 1) * qb - d * tk)
            if rows > 0:
                crosses = d * tk + rows - 1 > b * qb
                tasks.append((first_diag + d, ci, rows, d * tk if crosses else None))
    run(tasks, ms, None)

    d2 = 2 * DIFF_HEAD_DIM
    lam = (jnp.exp(jnp.sum(lq1_ref[...] * lk1_ref[...], axis=-1, keepdims=True))
           - jnp.exp(jnp.sum(lq2_ref[...] * lk2_ref[...], axis=-1, keepdims=True)) + LAM_INIT)
    a0 = acc_sc[0]
    a1 = acc_sc[1]
    o = a0[0:d2] / a0[d2:d2 + 1] - lam * (a1[0:d2] / a1[d2:d2 + 1])
    ms_o = jnp.mean(o * o, axis=0, keepdims=True)
    o = o * lax.rsqrt(ms_o + SUBLN_EPS) * sub_ref[...] * (1.0 - LAM_INIT)
    o_ref[...] = o.T.astype(o_ref.dtype)


def _diff_attention(qp, kp, vt, lq1, lk1, lq2, lk2, subln):
    S = qp.shape[0]
    tq = ATT_TQ
    d2 = 2 * DIFF_HEAD_DIM
    slopes = jnp.array(_alibi_slopes(), F32)
    vec = lambda z: z.reshape(1, -1).astype(F32)
    resident = functools.partial(pl.BlockSpec, pipeline_mode=pl.Buffered(1))
    return pl.pallas_call(
        _attn_kernel,
        grid=(DIFF_HEADS, S // tq),
        in_specs=[
            pl.BlockSpec(memory_space=pltpu.SMEM),
            pl.BlockSpec((tq, 2 * QK_PAD), lambda h, i: (i, h)),
            resident((S, 2 * QK_PAD), lambda h, i: (0, h)),
            resident((None, S // ATT_TK, VT_ROWS, ATT_TK), lambda h, i: (h, 0, 0, 0)),
            _const_spec((1, DIFF_HEAD_DIM)), _const_spec((1, DIFF_HEAD_DIM)),
            _const_spec((1, DIFF_HEAD_DIM)), _const_spec((1, DIFF_HEAD_DIM)),
            _const_spec((d2, 1)),
        ],
        out_specs=pl.BlockSpec((tq, d2), lambda h, i: (i, h)),
        out_shape=jax.ShapeDtypeStruct((S, DIFF_DIM), BF16),
        scratch_shapes=[pltpu.VMEM((2, VT_ROWS, tq), F32),
                        pltpu.VMEM((ATT_LOOKAHEAD, ATT_TK, ATT_QB), F32)],
        compiler_params=pltpu.CompilerParams(
            dimension_semantics=("arbitrary", "arbitrary"), vmem_limit_bytes=VMEM_LIMIT),
        name="diff_attention",
    )(slopes, qp, kp, vt, vec(lq1), vec(lk1), vec(lq2), vec(lk2),
      subln.reshape(d2, 1).astype(F32))


def _gelu_tanh(x):
    return 0.5 * x * (1.0 + jnp.tanh(math.sqrt(2.0 / math.pi) * (x + 0.044715 * (x * x * x))))


def _ffn_kernel(x_ref, yr_ref, yo_ref, wo_ref, gpost_ref, gpre_ref, wup_ref, cw_ref, cb_ref,
                wdn_ref, gffn_ref, out_ref, tail_sc, hn_sc, acc_sc):
    tm = FFN_ROWS
    cw = FFN_CHUNK

    @pl.when(pl.program_id(0) == 0)
    def _():
        tail_sc[...] = jnp.zeros(tail_sc.shape, F32)

    def rms(z, g_row):
        ms = jnp.mean(z * z, axis=-1, keepdims=True)
        return z * lax.rsqrt(ms + NORM_EPS) * g_row

    half = RWKV_DIM
    mixed = _dot(yr_ref[...], wo_ref[0:half, :]) + _dot(yo_ref[...], wo_ref[half:2 * half, :])
    h = x_ref[...] + rms(mixed, gpost_ref[...])
    hn_sc[...] = rms(h, gpre_ref[...]).astype(BF16)
    acc_sc[...] = jnp.zeros(acc_sc.shape, F32)

    sub8 = lax.broadcasted_iota(jnp.int32, (1, 8, cw), 1)

    def up(col0):
        return _dot(hn_sc[...], wup_ref[:, col0:col0 + cw])

    def conv(u, col0):
        tail = tail_sc[:, col0:col0 + cw]
        tail_sc[:, col0:col0 + cw] = u[tm - 8:tm]
        def shifted(k):
            rot = pltpu.roll(jnp.concatenate([tail, u], axis=0).reshape(tm // 8 + 1, 8, cw), k, axis=1)
            return jnp.where(sub8 < k, rot[:-1], rot[1:]).reshape(tm, cw)

        w = cw_ref[:, col0:col0 + cw]
        return (shifted(2) * w[0:1] + shifted(1) * w[1:2] + u * w[2:3]) + cb_ref[:, col0:col0 + cw]

    n_chunks = D_FF // cw
    nxt = (up(0), up(D_FF))
    for c in range(n_chunks):
        u_gate, u_val = nxt
        if c + 1 < n_chunks:
            nxt = (up((c + 1) * cw), up(D_FF + (c + 1) * cw))
        gate = conv(u_gate, c * cw)
        val = conv(u_val, D_FF + c * cw)
        act = (_gelu_tanh(gate) * val).astype(BF16)
        acc_sc[...] += _dot(act, wdn_ref[c * cw:(c + 1) * cw, :])

    out_ref[...] = h + rms(acc_sc[...], gffn_ref[...])


def _out_and_ffn(x2, y_rwkv, y_attn, w_out, g_post, g_pre, w_up, conv_w, conv_b, w_down, g_ffn):
    S = x2.shape[0]
    tm = FFN_ROWS
    row = lambda z: z.reshape(1, -1).astype(F32)
    single = functools.partial(pl.BlockSpec, pipeline_mode=pl.Buffered(1))
    const1 = lambda shape: single(shape, lambda i: (0,) * len(shape))
    return pl.pallas_call(
        _ffn_kernel,
        grid=(S // tm,),
        in_specs=[
            pl.BlockSpec((tm, D_MODEL), lambda i: (i, 0)),
            pl.BlockSpec((tm, RWKV_DIM), lambda i: (i, 0)),
            pl.BlockSpec((tm, DIFF_DIM), lambda i: (i, 0)),
            const1((D_MODEL, D_MODEL)),
            _const_spec((1, D_MODEL)), _const_spec((1, D_MODEL)),
            const1((D_MODEL, 2 * D_FF)),
            _const_spec((CONV_WIDTH, 2 * D_FF)), _const_spec((1, 2 * D_FF)),
            const1((D_FF, D_MODEL)),
            _const_spec((1, D_MODEL)),
        ],
        out_specs=pl.BlockSpec((tm, D_MODEL), lambda i: (i, 0)),
        out_shape=jax.ShapeDtypeStruct((S, D_MODEL), F32),
        scratch_shapes=[
            pltpu.VMEM((8, 2 * D_FF), F32),
            pltpu.VMEM((tm, D_MODEL), BF16),
            pltpu.VMEM((tm, D_MODEL), F32),
        ],
        compiler_params=pltpu.CompilerParams(
            dimension_semantics=("arbitrary",), vmem_limit_bytes=VMEM_LIMIT),
        name="out_ffn",
    )(x2, y_rwkv, y_attn, w_out.astype(BF16), row(g_post), row(g_pre), w_up.astype(BF16),
      conv_w.astype(F32), row(conv_b), w_down.astype(BF16), row(g_ffn))


def kernel(x, ln_attn_pre, w_in, mu_shift, w_decay_up, w_decay0, w_iclr_up, w_iclr0, w_gate_up,
           k_k, k_a, r_k, ln_x_w, ln_x_b, lambda_q1, lambda_k1, lambda_q2, lambda_k2, diff_subln,
           w_out, ln_attn_post, ln_ffn_pre, w_up, conv_w, conv_b, w_down, ln_ffn_post):
    B, S, _ = x.shape
    assert B == 1 and w_in.shape[0] == 1, "single batch, depth 1"
    assert S % POS_BLOCK == 0
    x2 = x[0]
    xs, qp, kp, vt = _in_projection(x2, ln_attn_pre[0], mu_shift[0], w_in[0])
    y_rwkv = _rwkv_mix(xs, w_decay_up[0], w_decay0[0], w_iclr_up[0], w_iclr0[0],
                       w_gate_up[0], k_k[0], k_a[0], r_k[0], ln_x_w[0], ln_x_b[0])
    y_attn = _diff_attention(qp, kp, vt, lambda_q1[0], lambda_k1[0], lambda_q2[0], lambda_k2[0],
                             diff_subln[0])
    out = _out_and_ffn(x2, y_rwkv, y_attn, w_out[0], ln_attn_post[0], ln_ffn_pre[0], w_up[0],
                       conv_w[0], conv_b[0], w_down[0], ln_ffn_post[0])
    return out[None]
```

```python
import functools
import math

import jax
import jax.numpy as jnp
import numpy as np
from jax import lax
from jax.experimental import pallas as pl
from jax.experimental.pallas import tpu as pltpu

F32 = jnp.float32
BF16 = jnp.bfloat16

D_MODEL = 1024
RWKV_HEADS = 8
RWKV_HEAD_DIM = 64
RWKV_DIM = RWKV_HEADS * RWKV_HEAD_DIM
LORA_W = 64
LORA_A = 64
LORA_G = 128
DIFF_HEADS = 4
DIFF_HEAD_DIM = 64
DIFF_DIM = DIFF_HEADS * 2 * DIFF_HEAD_DIM
RWKV_COLS = 3 * RWKV_DIM + LORA_W + LORA_A + LORA_G
D_FF = 2816
CONV_WIDTH = 3
DECAY_SCALE = math.exp(-0.5)
RWKV_LN_EPS = 64e-5
NORM_EPS = 1e-6
SUBLN_EPS = 1e-5
NEG_INF = -1e30
LAM_INIT = 0.8 - 0.6 * math.exp(-0.3 * 0)

LANES = 128
VMEM_LIMIT = 56 * 1024 * 1024

ATT_TQ = 1024
ATT_TK = 512
ATT_QB = 256
ATT_LOOKAHEAD = 3
POS_BLOCK = 1024
POS_SPLIT = 256
N_MAPS = 2 * DIFF_HEADS
QK_PAD = 128
VT_ROWS = 144

CHUNK = 64
RWKV_ROWS = 256
N_PAIRS = RWKV_HEADS // 2

PROJ_ROWS = 512
FFN_ROWS = 512
FFN_CHUNK = 256


def _dot(a, b):
    return jnp.dot(a, b, preferred_element_type=F32)


def _dot_nt(a, b):
    return lax.dot_general(a, b, (((1,), (1,)), ((), ())), preferred_element_type=F32)


def _dot_tn(a, b):
    return lax.dot_general(a, b, (((0,), (0,)), ((), ())), preferred_element_type=F32)


def _const_spec(shape):
    nd = len(shape)
    return pl.BlockSpec(shape, lambda *_: (0,) * nd)


def _inproj_kernel(x_ref, g_ref, mu_ref, w_ref, wvt_ref, qpat_ref, kpat_ref, vtpat_ref,
                   rw_ref, q_ref, k_ref, vt_ref, prev_sc):
    tm = PROJ_ROWS

    @pl.when(pl.program_id(0) == 0)
    def _():
        prev_sc[...] = jnp.zeros(prev_sc.shape, F32)

    x = x_ref[...]
    ms = jnp.mean(x * x, axis=-1, keepdims=True)
    xn = (x * lax.rsqrt(ms + NORM_EPS) * g_ref[...]).astype(BF16)
    c0 = RWKV_COLS
    c1 = c0 + DIFF_DIM
    c2 = c1 + DIFF_DIM
    p = _dot(xn, w_ref[:, 0:c0])
    row = lax.broadcasted_iota(jnp.int32, (8, RWKV_COLS), 0)
    shifted = pltpu.roll(p, 1, axis=0)
    first = jnp.where(row == 0, prev_sc[...], shifted[0:8])
    shifted = jnp.concatenate([first, shifted[8:]], axis=0)
    prev_sc[...] = jnp.broadcast_to(p[tm - 1:tm, :], prev_sc.shape)
    rw_ref[...] = p + (shifted - p) * mu_ref[...]

    low_half = lax.broadcasted_iota(jnp.int32, (tm, LANES), 1) < DIFF_HEAD_DIM

    def spread(z, pat_ref, out_ref):
        for j in range(DIFF_DIM // LANES):
            two_maps = z[:, j * LANES:(j + 1) * LANES]
            for odd in range(2):
                m = 2 * j + odd
                piece = pltpu.roll(two_maps, DIFF_HEAD_DIM, axis=1) if odd else two_maps
                slot = slice(m * QK_PAD, (m + 1) * QK_PAD)
                out_ref[:, slot] = (jnp.where(low_half, piece, 0.0) + pat_ref[:, slot]).astype(BF16)

    spread(_dot(xn, w_ref[:, c0:c1]), qpat_ref, q_ref)
    spread(_dot(xn, w_ref[:, c1:c2]), kpat_ref, k_ref)
    vt =(_dot_nt(wvt_ref[...], xn) + vtpat_ref[...]).astype(BF16)
    for h in range(DIFF_HEADS):
        vt_ref[h, 0] = vt[h * VT_ROWS:(h + 1) * VT_ROWS]


def _alibi_slopes():
    return [2.0 ** (-8.0 * (i + 1) / DIFF_HEADS) for i in range(DIFF_HEADS)]


def _alibi_patterns():
    pos = np.arange(POS_BLOCK)
    lo = (pos % POS_SPLIT).astype(np.float32)
    hi = (pos - pos % POS_SPLIT).astype(np.float32)
    qpat = np.zeros((POS_BLOCK, N_MAPS, QK_PAD), np.float32)
    kpat = np.zeros((POS_BLOCK, N_MAPS, QK_PAD), np.float32)
    d = DIFF_HEAD_DIM
    for m in range(N_MAPS):
        slope = _alibi_slopes()[m // 2]
        qpat[:, m, d + 0] = -slope * lo
        qpat[:, m, d + 1] = -slope * hi
        qpat[:, m, d + 2:d + 4] = 1.0
        kpat[:, m, d + 0:d + 2] = 1.0
        kpat[:, m, d + 2] = slope * lo
        kpat[:, m, d + 3] = slope * hi
    vtpat = np.zeros((DIFF_HEADS, VT_ROWS, PROJ_ROWS), np.float32)
    vtpat[:, 2 * d, :] = 1.0
    return (jnp.asarray(qpat.reshape(POS_BLOCK, N_MAPS * QK_PAD), BF16),
            jnp.asarray(kpat.reshape(POS_BLOCK, N_MAPS * QK_PAD), BF16),
            jnp.asarray(vtpat.reshape(DIFF_HEADS * VT_ROWS, PROJ_ROWS), BF16))


def _pack_w_in(w_in):
    d = DIFF_HEAD_DIM
    w_rw = w_in[:, :RWKV_COLS]
    wq = w_in[:, RWKV_COLS:RWKV_COLS + DIFF_DIM] * (d ** -0.5)
    wk = w_in[:, RWKV_COLS + DIFF_DIM:RWKV_COLS + 2 * DIFF_DIM]
    wv = w_in[:, RWKV_COLS + 2 * DIFF_DIM:].reshape(D_MODEL, DIFF_HEADS, 2 * d)
    wvt = jnp.pad(wv.transpose(1, 2, 0), ((0, 0), (0, VT_ROWS - 2 * d), (0, 0)))
    return (jnp.concatenate([w_rw, wq, wk], axis=1).astype(BF16),
            wvt.reshape(DIFF_HEADS * VT_ROWS, D_MODEL).astype(BF16))


def _in_projection(x2, g, mu, w_in):
    S = x2.shape[0]
    tm = PROJ_ROWS
    assert tm == ATT_TK, "the transposed value blocks are consumed one per attention kv step"
    w, wvt = _pack_w_in(w_in)
    qpat, kpat, vtpat = _alibi_patterns()
    nq = N_MAPS * QK_PAD
    rep = POS_BLOCK // tm
    single = functools.partial(pl.BlockSpec, pipeline_mode=pl.Buffered(1))
    return pl.pallas_call(
        _inproj_kernel,
        grid=(S // tm,),
        in_specs=[
            pl.BlockSpec((tm, D_MODEL), lambda i: (i, 0)),
            _const_spec((1, D_MODEL)),
            _const_spec((1, RWKV_COLS)),
            single(w.shape, lambda i: (0, 0)),
            single(wvt.shape, lambda i: (0, 0)),
            pl.BlockSpec((tm, nq), lambda i: (i % rep, 0)),
            pl.BlockSpec((tm, nq), lambda i: (i % rep, 0)),
            _const_spec(vtpat.shape),
        ],
        out_specs=[
            pl.BlockSpec((tm, RWKV_COLS), lambda i: (i, 0)),
            pl.BlockSpec((tm, nq), lambda i: (i, 0)),
            pl.BlockSpec((tm, nq), lambda i: (i, 0)),
            pl.BlockSpec((DIFF_HEADS, 1, VT_ROWS, tm), lambda i: (0, i, 0, 0)),
        ],
        out_shape=[
            jax.ShapeDtypeStruct((S, RWKV_COLS), F32),
            jax.ShapeDtypeStruct((S, nq), BF16),
            jax.ShapeDtypeStruct((S, nq), BF16),
            jax.ShapeDtypeStruct((DIFF_HEADS, S // tm, VT_ROWS, tm), BF16),
        ],
        scratch_shapes=[pltpu.VMEM((8, RWKV_COLS), F32)],
        compiler_params=pltpu.CompilerParams(
            dimension_semantics=("arbitrary",), vmem_limit_bytes=VMEM_LIMIT),
        name="in_projection",
    )(x2, g.reshape(1, D_MODEL), mu.reshape(1, RWKV_COLS).astype(F32), w, wvt, qpat, kpat, vtpat)


def _rwkv_kernel(xs_ref, wwa_ref, wd0_ref, wa0_ref, wg_ref, kk_ref, ka_ref, rk_ref,
                 lnw_ref, lnb_ref, hsum_ref, cum_ref, y_ref, state_sc, ych_sc):
    T = RWKV_ROWS
    C = CHUNK
    D = RWKV_DIM

    @pl.when(pl.program_id(0) == 0)
    def _():
        state_sc[...] = jnp.zeros(state_sc.shape, F32)

    r = xs_ref[:, 0:D]
    k = xs_ref[:, D:2 * D]
    v = xs_ref[:, 2 * D:3 * D]
    wa = xs_ref[:, 3 * D:3 * D + LANES]
    g_lo = xs_ref[:, 3 * D + LANES:3 * D + 2 * LANES]

    lane = lax.broadcasted_iota(jnp.int32, (T, LANES), 1)
    wa_act = jnp.where(lane < LORA_W, jnp.tanh(wa), wa).astype(BF16)
    pre = _dot(wa_act, wwa_ref[...])
    lw = -DECAY_SCALE * jax.nn.sigmoid(wd0_ref[...] + pre[:, 0:D])
    a = jax.nn.sigmoid(wa0_ref[...] + pre[:, D:2 * D])
    g = _dot(jax.nn.sigmoid(g_lo).astype(BF16), wg_ref[...])

    def head_sum(z):
        zb = z.astype(BF16)
        w = hsum_ref.shape[0]
        return jnp.concatenate([_dot(zb[:, o:o + w], hsum_ref[...]) for o in range(0, D, w)], axis=1)

    kk = k * kk_ref[...]
    n2 = head_sum(kk * kk)
    kk = kk * jnp.minimum(lax.rsqrt(n2), 1e12)
    k2 = k * (1.0 + (a - 1.0) * ka_ref[...])
    bonus = head_sum(r * k2 * rk_ref[...])

    lw_hi = lw.astype(BF16)
    lw_lo = (lw - lw_hi.astype(F32)).astype(BF16)
    cum = _dot(cum_ref[...], lw_hi) + _dot(cum_ref[...], lw_lo)
    L = cum[0:T]
    LC = cum[T:2 * T]
    e_fwd = jnp.exp(L)
    e_prev = jnp.exp(L - lw)
    e_inv = jnp.exp(-L)
    e_end = jnp.exp(LC - L)
    p_end = jnp.exp(LC)
    kka = kk * a
    r_t = (r * e_fwd).astype(BF16)
    al_t = (-kk * e_prev)
    k_t = (k2 * e_inv).astype(BF16)
    be_t = (kka * e_inv).astype(BF16)
    k_e = (k2 * e_end).astype(BF16)
    be_e = (kka * e_end).astype(BF16)
    v_b = v.astype(BF16)

    rr = lax.broadcasted_iota(jnp.int32, (2 * C, LANES), 0)
    ll = lax.broadcasted_iota(jnp.int32, (2 * C, LANES), 1)
    bd_mask = (rr < C) == (ll < C)
    tri_r = lax.broadcasted_iota(jnp.int32, (C, LANES), 0)
    tri_l = lax.broadcasted_iota(jnp.int32, (C, LANES), 1) & (C - 1)
    strict = tri_l < tri_r
    incl = tri_l <= tri_r

    def bd(z):
        zb = z.astype(BF16)
        return jnp.where(bd_mask, jnp.concatenate([zb, zb], axis=0), jnp.zeros((), BF16))

    n_chunks = T // C
    pairs = range(N_PAIRS)
    blocks = [(c, pr) for c in range(n_chunks) for pr in pairs]
    blk = {(c, pr): (slice(c * C, (c + 1) * C), slice(pr * LANES, (pr + 1) * LANES))
           for c, pr in blocks}

    gram = {}
    for key in blocks:
        sl = blk[key]
        xs_c = jnp.concatenate([al_t[sl].astype(BF16), r_t[sl]], axis=0)
        y_bd = jnp.concatenate([bd(be_t[sl]), bd(k_t[sl])], axis=0)
        gram[key] = _dot_nt(xs_c, y_bd)
    a_pow = {key: jnp.where(strict, gram[key][0:C, 0:LANES], 0.0) for key in blocks}
    a_r = {key: jnp.concatenate([jnp.where(incl, gram[key][C:2 * C, 0:LANES], 0.0),
                                 jnp.where(incl, gram[key][C:2 * C, LANES:2 * LANES], 0.0)],
                                axis=1).astype(BF16) for key in blocks}
    v_bd = {key: bd(v_b[blk[key]]) for key in blocks}

    x_u = {key: al_t[blk[key]] for key in blocks}
    x_0 = {key: _dot(jnp.where(strict, gram[key][0:C, LANES:2 * LANES], 0.0).astype(BF16), v_bd[key])
           for key in blocks}
    n_steps = int(math.log2(C))
    for step in range(n_steps):
        a_pow_b = {key: a_pow[key].astype(BF16) for key in blocks}
        for key in blocks:
            rhs = jnp.concatenate([bd(x_u[key]), bd(x_0[key])], axis=1)
            upd = _dot(a_pow_b[key], rhs)
            x_u[key] = x_u[key] + upd[:, 0:LANES]
            x_0[key] = x_0[key] + upd[:, LANES:2 * LANES]
        if step + 1 < n_steps:
            a_pow = {key: _dot(a_pow_b[key], bd(a_pow[key])) for key in blocks}

    w_mat = {}
    n_mat = {}
    for key in blocks:
        sl = blk[key]
        w_mat[key] = jnp.where(bd_mask, _dot_tn(x_u[key].astype(BF16), be_e[sl]), 0.0).astype(BF16)
        vu0 = jnp.concatenate([v_b[sl], x_0[key].astype(BF16)], axis=0)
        kb = jnp.concatenate([k_e[sl], be_e[sl]], axis=0)
        n_mat[key] = jnp.where(bd_mask, _dot_tn(vu0, kb), 0.0)

    def emit_outputs(c, u, y_s):
        for pr in pairs:
            ych_sc[blk[c, pr]] = y_s[pr] + _dot(
                a_r[c, pr], jnp.concatenate([bd(u[pr]), v_bd[c, pr]], axis=0))

    st = {pr: state_sc[pr] for pr in pairs}
    pending = None
    for c in range(n_chunks):
        st_b = {pr: st[pr].astype(BF16) for pr in pairs}
        st = {pr: st[pr] * p_end[c * C:c * C + 1, blk[c, pr][1]] + _dot(st_b[pr], w_mat[c, pr])
              + n_mat[c, pr] for pr in pairs}
        if pending is not None:
            emit_outputs(*pending)
        u = {pr: _dot_nt(x_u[c, pr].astype(BF16), st_b[pr]) + x_0[c, pr] for pr in pairs}
        y_s = {pr: _dot_nt(r_t[blk[c, pr]], st_b[pr]) for pr in pairs}
        pending = (c, u, y_s)
    emit_outputs(*pending)
    for pr in pairs:
        state_sc[pr] = st[pr]

    y = ych_sc[...] + bonus * v
    inv_n = 1.0 / RWKV_HEAD_DIM
    mean = head_sum(y) * inv_n
    yc = y - mean
    var = head_sum(yc * yc) * inv_n
    yn = yc * lax.rsqrt(var + RWKV_LN_EPS)
    y_ref[...] = ((yn * lnw_ref[...] + lnb_ref[...]) * g).astype(y_ref.dtype)


def _rwkv_mix(xs, w_decay_up, w_decay0, w_iclr_up, w_iclr0, w_gate_up, k_k, k_a, r_k,
              ln_x_w, ln_x_b):
    S = xs.shape[0]
    T = RWKV_ROWS
    D = RWKV_DIM
    wwa = jnp.zeros((LANES, 2 * D), F32)
    wwa = wwa.at[0:LORA_W, 0:D].set(w_decay_up).at[LORA_W:LORA_W + LORA_A, D:2 * D].set(w_iclr_up)
    hid = np.arange(2 * LANES) // RWKV_HEAD_DIM
    hsum = jnp.asarray(hid[:, None] == hid[None, :], BF16)
    t = np.arange(T)
    same = (t[:, None] // CHUNK) == (t[None, :] // CHUNK)
    cum = jnp.asarray(np.concatenate([same & (t[None, :] <= t[:, None]), same], axis=0), BF16)
    row = lambda z: z.reshape(1, -1).astype(F32)
    args = (xs, wwa.astype(BF16), row(w_decay0), row(w_iclr0), w_gate_up.astype(BF16),
            row(k_k), row(k_a), row(r_k), row(ln_x_w), row(ln_x_b), hsum, cum)
    in_specs = [pl.BlockSpec((T, RWKV_COLS), lambda i: (i, 0))]
    in_specs += [_const_spec(z.shape) for z in args[1:]]
    return pl.pallas_call(
        _rwkv_kernel,
        grid=(S // T,),
        in_specs=in_specs,
        out_specs=pl.BlockSpec((T, D), lambda i: (i, 0)),
        out_shape=jax.ShapeDtypeStruct((S, D), BF16),
        scratch_shapes=[
            pltpu.VMEM((N_PAIRS, LANES, LANES), F32),
            pltpu.VMEM((T, D), F32),
        ],
        compiler_params=pltpu.CompilerParams(
            dimension_semantics=("arbitrary",), vmem_limit_bytes=VMEM_LIMIT),
        name="rwkv7_mix",
    )(*args)


def _attn_kernel(slopes_ref, q_ref, k_ref, vt_ref, lq1_ref, lk1_ref, lq2_ref, lk2_ref, sub_ref,
                 o_ref, acc_sc, s_sc):
    tq = ATT_TQ
    tk = ATT_TK
    qb = ATT_QB
    nb = tq // qb
    kv_per_q = tq // tk
    h = pl.program_id(0)
    i = pl.program_id(1)
    slope = slopes_ref[h]
    q_base = ((i * tq) // POS_BLOCK) * POS_BLOCK

    acc_sc[...] = jnp.zeros(acc_sc.shape, F32)
    chains = [(c, b) for c in range(2) for b in range(nb)]
    qs = [q_ref[b * qb:(b + 1) * qb, c * QK_PAD:(c + 1) * QK_PAD] for c, b in chains]
    n = len(chains)
    ahead = ATT_LOOKAHEAD

    def scores(j, ci, rows):
        c, _ = chains[ci]
        k0 = pl.multiple_of(j * tk, tk)
        kc = k_ref[pl.ds(k0, rows), c * QK_PAD:(c + 1) * QK_PAD]
        return _dot_nt(kc, qs[ci])

    def accumulate(j, ci, s, m_old, kv_offset):
        c, b = chains[ci]
        rows = s.shape[0]
        k_base = ((j * tk) // POS_BLOCK) * POS_BLOCK
        shift = slope * (q_base - k_base).astype(F32)
        if kv_offset is not None:
            kv_i = lax.broadcasted_iota(jnp.int32, (rows, qb), 0) + kv_offset
            q_i = lax.broadcasted_iota(jnp.int32, (rows, qb), 1) + b * qb
            s = jnp.where(kv_i <= q_i, s, NEG_INF)
        m_new = jnp.maximum(m_old, jnp.max(s, axis=0, keepdims=True) - shift)
        p = jnp.exp(s - (m_new + shift)).astype(BF16)
        alpha = jnp.exp(m_old - m_new)
        cols = slice(b * qb, (b + 1) * qb)
        acc_sc[c, :, cols] = alpha * acc_sc[c, :, cols] + _dot(vt_ref[j, :, 0:rows], p)
        return m_new

    def run(tasks, ms, next_tile):
        ms = list(ms)
        early = {}
        for t, (j, ci, rows, kv_offset) in enumerate(tasks):
            nxt = t + ahead
            if nxt < len(tasks):
                early[nxt] = scores(tasks[nxt][0], tasks[nxt][1], tasks[nxt][2])
            elif next_tile is not None:
                s_sc[nxt - len(tasks)] = scores(next_tile, nxt - len(tasks), tk)
            s = s_sc[t, 0:rows] if t < ahead else early.pop(t)
            ms[ci] = accumulate(j, ci, s, ms[ci], kv_offset)
        return tuple(ms)

    for t in range(ahead):
        s_sc[t] = scores(0, t, tk)
    ms0 = tuple(jnp.full((1, qb), NEG_INF, F32) for _ in chains)
    first_diag = i * kv_per_q
    ms = lax.fori_loop(
        0, i,
        lambda g, ms: run([(g * kv_per_q + d, ci, tk, None) for d in range(kv_per_q) for ci in range(n)],
                          ms, (g + 1) * kv_per_q), ms0)

    tasks = []
    for d in range(kv_per_q):
        for ci, (c, b) in enumerate(chains):
            rows = min(tk, (b + 1) * qb - d * tk)
            if rows > 0:
                crosses = d * tk + rows - 1 > b * qb
                tasks.append((first_diag + d, ci, rows, d * tk if crosses else None))
    run(tasks, ms, None)

    d2 = 2 * DIFF_HEAD_DIM
    lam = (jnp.exp(jnp.sum(lq1_ref[...] * lk1_ref[...], axis=-1, keepdims=True))
           - jnp.exp(jnp.sum(lq2_ref[...] * lk2_ref[...], axis=-1, keepdims=True)) + LAM_INIT)
    a0 = acc_sc[0]
    a1 = acc_sc[1]
    o = a0[0:d2] / a0[d2:d2 + 1] - lam * (a1[0:d2] / a1[d2:d2 + 1])
    ms_o = jnp.mean(o * o, axis=0, keepdims=True)
    o = o * lax.rsqrt(ms_o + SUBLN_EPS) * sub_ref[...] * (1.0 - LAM_INIT)
    o_ref[...] = o.T.astype(o_ref.dtype)


def _diff_attention(qp, kp, vt, lq1, lk1, lq2, lk2, subln):
    S = qp.shape[0]
    tq = ATT_TQ
    d2 = 2 * DIFF_HEAD_DIM
    slopes = jnp.array(_alibi_slopes(), F32)
    vec = lambda z: z.reshape(1, -1).astype(F32)
    resident = pl.BlockSpec
    return pl.pallas_call(
        _attn_kernel,
        grid=(DIFF_HEADS, S // tq),
        in_specs=[
            pl.BlockSpec(memory_space=pltpu.SMEM),
            pl.BlockSpec((tq, 2 * QK_PAD), lambda h, i: (i, h)),
            resident((S, 2 * QK_PAD), lambda h, i: (0, h)),
            resident((None, S // ATT_TK, VT_ROWS, ATT_TK), lambda h, i: (h, 0, 0, 0)),
            _const_spec((1, DIFF_HEAD_DIM)), _const_spec((1, DIFF_HEAD_DIM)),
            _const_spec((1, DIFF_HEAD_DIM)), _const_spec((1, DIFF_HEAD_DIM)),
            _const_spec((d2, 1)),
        ],
        out_specs=pl.BlockSpec((tq, d2), lambda h, i: (i, h)),
        out_shape=jax.ShapeDtypeStruct((S, DIFF_DIM), BF16),
        scratch_shapes=[pltpu.VMEM((2, VT_ROWS, tq), F32),
                        pltpu.VMEM((ATT_LOOKAHEAD, ATT_TK, ATT_QB), F32)],
        compiler_params=pltpu.CompilerParams(
            dimension_semantics=("arbitrary", "arbitrary"), vmem_limit_bytes=VMEM_LIMIT),
        name="diff_attention",
    )(slopes, qp, kp, vt, vec(lq1), vec(lk1), vec(lq2), vec(lk2),
      subln.reshape(d2, 1).astype(F32))


def _gelu_tanh(x):
    return 0.5 * x * (1.0 + jnp.tanh(math.sqrt(2.0 / math.pi) * (x + 0.044715 * (x * x * x))))


def _ffn_kernel(x_ref, yr_ref, yo_ref, wo_ref, gpost_ref, gpre_ref, wup_ref, cw_ref, cb_ref,
                wdn_ref, gffn_ref, out_ref, tail_sc, hn_sc, acc_sc):
    tm = FFN_ROWS
    cw = FFN_CHUNK

    @pl.when(pl.program_id(0) == 0)
    def _():
        tail_sc[...] = jnp.zeros(tail_sc.shape, F32)

    def rms(z, g_row):
        ms = jnp.mean(z * z, axis=-1, keepdims=True)
        return z * lax.rsqrt(ms + NORM_EPS) * g_row

    half = RWKV_DIM
    mixed = _dot(yr_ref[...], wo_ref[0:half, :]) + _dot(yo_ref[...], wo_ref[half:2 * half, :])
    h = x_ref[...] + rms(mixed, gpost_ref[...])
    hn_sc[...] = rms(h, gpre_ref[...]).astype(BF16)
    acc_sc[...] = jnp.zeros(acc_sc.shape, F32)

    sub8 = lax.broadcasted_iota(jnp.int32, (1, 8, cw), 1)

    def up(col0):
        return _dot(hn_sc[...], wup_ref[:, col0:col0 + cw])

    def conv(u, col0):
        tail = tail_sc[:, col0:col0 + cw]
        tail_sc[:, col0:col0 + cw] = u[tm - 8:tm]
        def shifted(k):
            rot = pltpu.roll(jnp.concatenate([tail, u], axis=0).reshape(tm // 8 + 1, 8, cw), k, axis=1)
            return jnp.where(sub8 < k, rot[:-1], rot[1:]).reshape(tm, cw)

        w = cw_ref[:, col0:col0 + cw]
        return (shifted(2) * w[0:1] + shifted(1) * w[1:2] + u * w[2:3]) + cb_ref[:, col0:col0 + cw]

    n_chunks = D_FF // cw
    nxt = (up(0), up(D_FF))
    for c in range(n_chunks):
        u_gate, u_val = nxt
        if c + 1 < n_chunks:
            nxt = (up((c + 1) * cw), up(D_FF + (c + 1) * cw))
        gate = conv(u_gate, c * cw)
        val = conv(u_val, D_FF + c * cw)
        act = (_gelu_tanh(gate) * val).astype(BF16)
        acc_sc[...] += _dot(act, wdn_ref[c * cw:(c + 1) * cw, :])

    out_ref[...] = h + rms(acc_sc[...], gffn_ref[...])


def _out_and_ffn(x2, y_rwkv, y_attn, w_out, g_post, g_pre, w_up, conv_w, conv_b, w_down, g_ffn):
    S = x2.shape[0]
    tm = FFN_ROWS
    row = lambda z: z.reshape(1, -1).astype(F32)
    single = functools.partial(pl.BlockSpec, pipeline_mode=pl.Buffered(1))
    const1 = lambda shape: single(shape, lambda i: (0,) * len(shape))
    return pl.pallas_call(
        _ffn_kernel,
        grid=(S // tm,),
        in_specs=[
            pl.BlockSpec((tm, D_MODEL), lambda i: (i, 0)),
            pl.BlockSpec((tm, RWKV_DIM), lambda i: (i, 0)),
            pl.BlockSpec((tm, DIFF_DIM), lambda i: (i, 0)),
            const1((D_MODEL, D_MODEL)),
            _const_spec((1, D_MODEL)), _const_spec((1, D_MODEL)),
            const1((D_MODEL, 2 * D_FF)),
            _const_spec((CONV_WIDTH, 2 * D_FF)), _const_spec((1, 2 * D_FF)),
            const1((D_FF, D_MODEL)),
            _const_spec((1, D_MODEL)),
        ],
        out_specs=pl.BlockSpec((tm, D_MODEL), lambda i: (i, 0)),
        out_shape=jax.ShapeDtypeStruct((S, D_MODEL), F32),
        scratch_shapes=[
            pltpu.VMEM((8, 2 * D_FF), F32),
            pltpu.VMEM((tm, D_MODEL), BF16),
            pltpu.VMEM((tm, D_MODEL), F32),
        ],
        compiler_params=pltpu.CompilerParams(
            dimension_semantics=("arbitrary",), vmem_limit_bytes=VMEM_LIMIT),
        name="out_ffn",
    )(x2, y_rwkv, y_attn, w_out.astype(BF16), row(g_post), row(g_pre), w_up.astype(BF16),
      conv_w.astype(F32), row(conv_b), w_down.astype(BF16), row(g_ffn))


def kernel(x, ln_attn_pre, w_in, mu_shift, w_decay_up, w_decay0, w_iclr_up, w_iclr0, w_gate_up,
           k_k, k_a, r_k, ln_x_w, ln_x_b, lambda_q1, lambda_k1, lambda_q2, lambda_k2, diff_subln,
           w_out, ln_attn_post, ln_ffn_pre, w_up, conv_w, conv_b, w_down, ln_ffn_post):
    B, S, _ = x.shape
    assert B == 1 and w_in.shape[0] == 1, "single batch, depth 1"
    assert S % POS_BLOCK == 0
    x2 = x[0]
    xs, qp, kp, vt = _in_projection(x2, ln_attn_pre[0], mu_shift[0], w_in[0])
    y_rwkv = _rwkv_mix(xs, w_decay_up[0], w_decay0[0], w_iclr_up[0], w_iclr0[0],
                       w_gate_up[0], k_k[0], k_a[0], r_k[0], ln_x_w[0], ln_x_b[0])
    y_attn = _diff_attention(qp, kp, vt, lambda_q1[0], lambda_k1[0], lambda_q2[0], lambda_k2[0],
                             diff_subln[0])
    out = _out_and_ffn(x2, y_rwkv, y_attn, w_out[0], ln_attn_post[0], ln_ffn_pre[0], w_up[0],
                       conv_w[0], conv_b[0], w_down[0], ln_ffn_post[0])
    return out[None]
```

```python
import functools
import math

import jax
import jax.numpy as jnp
import numpy as np
from jax import lax
from jax.experimental import pallas as pl
from jax.experimental.pallas import tpu as pltpu

F32 = jnp.float32
BF16 = jnp.bfloat16

D_MODEL = 1024
RWKV_HEADS = 8
RWKV_HEAD_DIM = 64
RWKV_DIM = RWKV_HEADS * RWKV_HEAD_DIM
LORA_W = 64
LORA_A = 64
LORA_G = 128
DIFF_HEADS = 4
DIFF_HEAD_DIM = 64
DIFF_DIM = DIFF_HEADS * 2 * DIFF_HEAD_DIM
RWKV_COLS = 3 * RWKV_DIM + LORA_W + LORA_A + LORA_G
D_FF = 2816
CONV_WIDTH = 3
DECAY_SCALE = math.exp(-0.5)
RWKV_LN_EPS = 64e-5
NORM_EPS = 1e-6
SUBLN_EPS = 1e-5
NEG_INF = -1e30
LAM_INIT = 0.8 - 0.6 * math.exp(-0.3 * 0)

LANES = 128
VMEM_LIMIT = 56 * 1024 * 1024

ATT_TQ = 1024
ATT_TK = 512
ATT_QB = 256
ATT_LOOKAHEAD = 3
POS_BLOCK = 1024
POS_SPLIT = 256
N_MAPS = 2 * DIFF_HEADS
QK_PAD = 128
VT_ROWS = 144

CHUNK = 64
RWKV_ROWS = 256
N_PAIRS = RWKV_HEADS // 2

PROJ_ROWS = 512
FFN_ROWS = 512
FFN_CHUNK = 256


def _dot(a, b):
    return jnp.dot(a, b, preferred_element_type=F32)


def _dot_nt(a, b):
    return lax.dot_general(a, b, (((1,), (1,)), ((), ())), preferred_element_type=F32)


def _dot_tn(a, b):
    return lax.dot_general(a, b, (((0,), (0,)), ((), ())), preferred_element_type=F32)


def _const_spec(shape):
    nd = len(shape)
    return pl.BlockSpec(shape, lambda *_: (0,) * nd)


def _inproj_kernel(x_ref, g_ref, mu_ref, w_ref, wvt_ref, qpat_ref, kpat_ref, vtpat_ref,
                   rw_ref, q_ref, k_ref, vt_ref, prev_sc):
    tm = PROJ_ROWS

    @pl.when(pl.program_id(0) == 0)
    def _():
        prev_sc[...] = jnp.zeros(prev_sc.shape, F32)

    x = x_ref[...]
    ms = jnp.mean(x * x, axis=-1, keepdims=True)
    xn = (x * lax.rsqrt(ms + NORM_EPS) * g_ref[...]).astype(BF16)
    c0 = RWKV_COLS
    c1 = c0 + DIFF_DIM
    c2 = c1 + DIFF_DIM
    p = _dot(xn, w_ref[:, 0:c0])
    row = lax.broadcasted_iota(jnp.int32, (8, RWKV_COLS), 0)
    shifted = pltpu.roll(p, 1, axis=0)
    first = jnp.where(row == 0, prev_sc[...], shifted[0:8])
    shifted = jnp.concatenate([first, shifted[8:]], axis=0)
    prev_sc[...] = jnp.broadcast_to(p[tm - 1:tm, :], prev_sc.shape)
    rw_ref[...] = p + (shifted - p) * mu_ref[...]

    low_half = lax.broadcasted_iota(jnp.int32, (tm, LANES), 1) < DIFF_HEAD_DIM

    def spread(z, pat_ref, out_ref):
        for j in range(DIFF_DIM // LANES):
            two_maps = z[:, j * LANES:(j + 1) * LANES]
            for odd in range(2):
                m = 2 * j + odd
                piece = pltpu.roll(two_maps, DIFF_HEAD_DIM, axis=1) if odd else two_maps
                slot = slice(m * QK_PAD, (m + 1) * QK_PAD)
                out_ref[:, slot] = (jnp.where(low_half, piece, 0.0) + pat_ref[:, slot]).astype(BF16)

    spread(_dot(xn, w_ref[:, c0:c1]), qpat_ref, q_ref)
    spread(_dot(xn, w_ref[:, c1:c2]), kpat_ref, k_ref)
    vt =(_dot_nt(wvt_ref[...], xn) + vtpat_ref[...]).astype(BF16)
    for h in range(DIFF_HEADS):
        vt_ref[h, 0] = vt[h * VT_ROWS:(h + 1) * VT_ROWS]


def _alibi_slopes():
    return [2.0 ** (-8.0 * (i + 1) / DIFF_HEADS) for i in range(DIFF_HEADS)]


def _alibi_patterns():
    pos = np.arange(POS_BLOCK)
    lo = (pos % POS_SPLIT).astype(np.float32)
    hi = (pos - pos % POS_SPLIT).astype(np.float32)
    qpat = np.zeros((POS_BLOCK, N_MAPS, QK_PAD), np.float32)
    kpat = np.zeros((POS_BLOCK, N_MAPS, QK_PAD), np.float32)
    d = DIFF_HEAD_DIM
    for m in range(N_MAPS):
        slope = _alibi_slopes()[m // 2]
        qpat[:, m, d + 0] = -slope * lo
        qpat[:, m, d + 1] = -slope * hi
        qpat[:, m, d + 2:d + 4] = 1.0
        kpat[:, m, d + 0:d + 2] = 1.0
        kpat[:, m, d + 2] = slope * lo
        kpat[:, m, d + 3] = slope * hi
    vtpat = np.zeros((DIFF_HEADS, VT_ROWS, PROJ_ROWS), np.float32)
    vtpat[:, 2 * d, :] = 1.0
    return (jnp.asarray(qpat.reshape(POS_BLOCK, N_MAPS * QK_PAD), BF16),
            jnp.asarray(kpat.reshape(POS_BLOCK, N_MAPS * QK_PAD), BF16),
            jnp.asarray(vtpat.reshape(DIFF_HEADS * VT_ROWS, PROJ_ROWS), BF16))


def _pack_w_in(w_in):
    d = DIFF_HEAD_DIM
    w_rw = w_in[:, :RWKV_COLS]
    wq = w_in[:, RWKV_COLS:RWKV_COLS + DIFF_DIM] * (d ** -0.5)
    wk = w_in[:, RWKV_COLS + DIFF_DIM:RWKV_COLS + 2 * DIFF_DIM]
    wv = w_in[:, RWKV_COLS + 2 * DIFF_DIM:].reshape(D_MODEL, DIFF_HEADS, 2 * d)
    wvt = jnp.pad(wv.transpose(1, 2, 0), ((0, 0), (0, VT_ROWS - 2 * d), (0, 0)))
    return (jnp.concatenate([w_rw, wq, wk], axis=1).astype(BF16),
            wvt.reshape(DIFF_HEADS * VT_ROWS, D_MODEL).astype(BF16))


def _in_projection(x2, g, mu, w_in):
    S = x2.shape[0]
    tm = PROJ_ROWS
    assert tm == ATT_TK, "the transposed value blocks are consumed one per attention kv step"
    w, wvt = _pack_w_in(w_in)
    qpat, kpat, vtpat = _alibi_patterns()
    nq = N_MAPS * QK_PAD
    rep = POS_BLOCK // tm
    single = functools.partial(pl.BlockSpec, pipeline_mode=pl.Buffered(1))
    return pl.pallas_call(
        _inproj_kernel,
        grid=(S // tm,),
        in_specs=[
            pl.BlockSpec((tm, D_MODEL), lambda i: (i, 0)),
            _const_spec((1, D_MODEL)),
            _const_spec((1, RWKV_COLS)),
            single(w.shape, lambda i: (0, 0)),
            single(wvt.shape, lambda i: (0, 0)),
            pl.BlockSpec((tm, nq), lambda i: (i % rep, 0)),
            pl.BlockSpec((tm, nq), lambda i: (i % rep, 0)),
            _const_spec(vtpat.shape),
        ],
        out_specs=[
            pl.BlockSpec((tm, RWKV_COLS), lambda i: (i, 0)),
            pl.BlockSpec((tm, nq), lambda i: (i, 0)),
            pl.BlockSpec((tm, nq), lambda i: (i, 0)),
            pl.BlockSpec((DIFF_HEADS, 1, VT_ROWS, tm), lambda i: (0, i, 0, 0)),
        ],
        out_shape=[
            jax.ShapeDtypeStruct((S, RWKV_COLS), F32),
            jax.ShapeDtypeStruct((S, nq), BF16),
            jax.ShapeDtypeStruct((S, nq), BF16),
            jax.ShapeDtypeStruct((DIFF_HEADS, S // tm, VT_ROWS, tm), BF16),
        ],
        scratch_shapes=[pltpu.VMEM((8, RWKV_COLS), F32)],
        compiler_params=pltpu.CompilerParams(
            dimension_semantics=("arbitrary",), vmem_limit_bytes=VMEM_LIMIT),
        name="in_projection",
    )(x2, g.reshape(1, D_MODEL), mu.reshape(1, RWKV_COLS).astype(F32), w, wvt, qpat, kpat, vtpat)


def _rwkv_kernel(xs_ref, wwa_ref, wd0_ref, wa0_ref, wg_ref, kk_ref, ka_ref, rk_ref,
                 lnw_ref, lnb_ref, hsum_ref, cum_ref, y_ref, state_sc, ych_sc):
    T = RWKV_ROWS
    C = CHUNK
    D = RWKV_DIM

    @pl.when(pl.program_id(0) == 0)
    def _():
        state_sc[...] = jnp.zeros(state_sc.shape, F32)

    r = xs_ref[:, 0:D]
    k = xs_ref[:, D:2 * D]
    v = xs_ref[:, 2 * D:3 * D]
    wa = xs_ref[:, 3 * D:3 * D + LANES]
    g_lo = xs_ref[:, 3 * D + LANES:3 * D + 2 * LANES]

    lane = lax.broadcasted_iota(jnp.int32, (T, LANES), 1)
    wa_act = jnp.where(lane < LORA_W, jnp.tanh(wa), wa).astype(BF16)
    pre = _dot(wa_act, wwa_ref[...])
    lw = -DECAY_SCALE * jax.nn.sigmoid(wd0_ref[...] + pre[:, 0:D])
    a = jax.nn.sigmoid(wa0_ref[...] + pre[:, D:2 * D])
    g = _dot(jax.nn.sigmoid(g_lo).astype(BF16), wg_ref[...])

    def head_sum(z):
        zb = z.astype(BF16)
        w = hsum_ref.shape[0]
        return jnp.concatenate([_dot(zb[:, o:o + w], hsum_ref[...]) for o in range(0, D, w)], axis=1)

    kk = k * kk_ref[...]
    n2 = head_sum(kk * kk)
    kk = kk * jnp.minimum(lax.rsqrt(n2), 1e12)
    k2 = k * (1.0 + (a - 1.0) * ka_ref[...])
    bonus = head_sum(r * k2 * rk_ref[...])

    lw_hi = lw.astype(BF16)
    lw_lo = (lw - lw_hi.astype(F32)).astype(BF16)
    cum = _dot(cum_ref[...], lw_hi) + _dot(cum_ref[...], lw_lo)
    L = cum[0:T]
    LC = cum[T:2 * T]
    e_fwd = jnp.exp(L)
    e_prev = jnp.exp(L - lw)
    e_inv = jnp.exp(-L)
    e_end = jnp.exp(LC - L)
    p_end = jnp.exp(LC)
    kka = kk * a
    r_t = (r * e_fwd).astype(BF16)
    al_t = (-kk * e_prev)
    k_t = (k2 * e_inv).astype(BF16)
    be_t = (kka * e_inv).astype(BF16)
    k_e = (k2 * e_end).astype(BF16)
    be_e = (kka * e_end).astype(BF16)
    v_b = v.astype(BF16)

    rr = lax.broadcasted_iota(jnp.int32, (2 * C, LANES), 0)
    ll = lax.broadcasted_iota(jnp.int32, (2 * C, LANES), 1)
    bd_mask = (rr < C) == (ll < C)
    tri_r = lax.broadcasted_iota(jnp.int32, (C, LANES), 0)
    tri_l = lax.broadcasted_iota(jnp.int32, (C, LANES), 1) & (C - 1)
    strict = tri_l < tri_r
    incl = tri_l <= tri_r

    def bd(z):
        zb = z.astype(BF16)
        return jnp.where(bd_mask, jnp.concatenate([zb, zb], axis=0), jnp.zeros((), BF16))

    n_chunks = T // C
    pairs = range(N_PAIRS)
    blocks = [(c, pr) for c in range(n_chunks) for pr in pairs]
    blk = {(c, pr): (slice(c * C, (c + 1) * C), slice(pr * LANES, (pr + 1) * LANES))
           for c, pr in blocks}

    gram = {}
    for key in blocks:
        sl = blk[key]
        xs_c = jnp.concatenate([al_t[sl].astype(BF16), r_t[sl]], axis=0)
        y_bd = jnp.concatenate([bd(be_t[sl]), bd(k_t[sl])], axis=0)
        gram[key] = _dot_nt(xs_c, y_bd)
    a_pow = {key: jnp.where(strict, gram[key][0:C, 0:LANES], 0.0) for key in blocks}
    a_r = {key: jnp.concatenate([jnp.where(incl, gram[key][C:2 * C, 0:LANES], 0.0),
                                 jnp.where(incl, gram[key][C:2 * C, LANES:2 * LANES], 0.0)],
                                axis=1).astype(BF16) for key in blocks}
    v_bd = {key: bd(v_b[blk[key]]) for key in blocks}

    x_u = {key: al_t[blk[key]] for key in blocks}
    x_0 = {key: _dot(jnp.where(strict, gram[key][0:C, LANES:2 * LANES], 0.0).astype(BF16), v_bd[key])
           for key in blocks}
    n_steps = int(math.log2(C))
    for step in range(n_steps):
        a_pow_b = {key: a_pow[key].astype(BF16) for key in blocks}
        for key in blocks:
            rhs = jnp.concatenate([bd(x_u[key]), bd(x_0[key])], axis=1)
            upd = _dot(a_pow_b[key], rhs)
            x_u[key] = x_u[key] + upd[:, 0:LANES]
            x_0[key] = x_0[key] + upd[:, LANES:2 * LANES]
        if step + 1 < n_steps:
            a_pow = {key: _dot(a_pow_b[key], bd(a_pow[key])) for key in blocks}

    w_mat = {}
    n_mat = {}
    for key in blocks:
        sl = blk[key]
        w_mat[key] = jnp.where(bd_mask, _dot_tn(x_u[key].astype(BF16), be_e[sl]), 0.0).astype(BF16)
        vu0 = jnp.concatenate([v_b[sl], x_0[key].astype(BF16)], axis=0)
        kb = jnp.concatenate([k_e[sl], be_e[sl]], axis=0)
        n_mat[key] = jnp.where(bd_mask, _dot_tn(vu0, kb), 0.0)

    def emit_outputs(c, u, y_s):
        for pr in pairs:
            ych_sc[blk[c, pr]] = y_s[pr] + _dot(
                a_r[c, pr], jnp.concatenate([bd(u[pr]), v_bd[c, pr]], axis=0))

    st = {pr: state_sc[pr] for pr in pairs}
    pending = None
    for c in range(n_chunks):
        st_b = {pr: st[pr].astype(BF16) for pr in pairs}
        st = {pr: st[pr] * p_end[c * C:c * C + 1, blk[c, pr][1]] + _dot(st_b[pr], w_mat[c, pr])
              + n_mat[c, pr] for pr in pairs}
        if pending is not None:
            emit_outputs(*pending)
        u = {pr: _dot_nt(x_u[c, pr].astype(BF16), st_b[pr]) + x_0[c, pr] for pr in pairs}
        y_s = {pr: _dot_nt(r_t[blk[c, pr]], st_b[pr]) for pr in pairs}
        pending = (c, u, y_s)
    emit_outputs(*pending)
    for pr in pairs:
        state_sc[pr] = st[pr]

    y = ych_sc[...] + bonus * v
    inv_n = 1.0 / RWKV_HEAD_DIM
    mean = head_sum(y) * inv_n
    yc = y - mean
    var = head_sum(yc * yc) * inv_n
    yn = yc * lax.rsqrt(var + RWKV_LN_EPS)
    y_ref[...] = ((yn * lnw_ref[...] + lnb_ref[...]) * g).astype(y_ref.dtype)


def _rwkv_mix(xs, w_decay_up, w_decay0, w_iclr_up, w_iclr0, w_gate_up, k_k, k_a, r_k,
              ln_x_w, ln_x_b):
    S = xs.shape[0]
    T = RWKV_ROWS
    D = RWKV_DIM
    wwa = jnp.zeros((LANES, 2 * D), F32)
    wwa = wwa.at[0:LORA_W, 0:D].set(w_decay_up).at[LORA_W:LORA_W + LORA_A, D:2 * D].set(w_iclr_up)
    hid = np.arange(2 * LANES) // RWKV_HEAD_DIM
    hsum = jnp.asarray(hid[:, None] == hid[None, :], BF16)
    t = np.arange(T)
    same = (t[:, None] // CHUNK) == (t[None, :] // CHUNK)
    cum = jnp.asarray(np.concatenate([same & (t[None, :] <= t[:, None]), same], axis=0), BF16)
    row = lambda z: z.reshape(1, -1).astype(F32)
    args = (xs, wwa.astype(BF16), row(w_decay0), row(w_iclr0), w_gate_up.astype(BF16),
            row(k_k), row(k_a), row(r_k), row(ln_x_w), row(ln_x_b), hsum, cum)
    in_specs = [pl.BlockSpec((T, RWKV_COLS), lambda i: (i, 0))]
    in_specs += [_const_spec(z.shape) for z in args[1:]]
    return pl.pallas_call(
        _rwkv_kernel,
        grid=(S // T,),
        in_specs=in_specs,
        out_specs=pl.BlockSpec((T, D), lambda i: (i, 0)),
        out_shape=jax.ShapeDtypeStruct((S, D), BF16),
        scratch_shapes=[
            pltpu.VMEM((N_PAIRS, LANES, LANES), F32),
            pltpu.VMEM((T, D), F32),
        ],
        compiler_params=pltpu.CompilerParams(
            dimension_semantics=("arbitrary",), vmem_limit_bytes=VMEM_LIMIT),
        name="rwkv7_mix",
    )(*args)


def _attn_kernel(slopes_ref, q_ref, k_ref, vt_ref, lq1_ref, lk1_ref, lq2_ref, lk2_ref, sub_ref,
                 o_ref, acc_sc, s_sc):
    tq = ATT_TQ
    tk = ATT_TK
    qb = ATT_QB
    nb = tq // qb
    kv_per_q = tq // tk
    h = pl.program_id(0)
    i = pl.program_id(1)
    slope = slopes_ref[h]
    q_base = ((i * tq) // POS_BLOCK) * POS_BLOCK

    acc_sc[...] = jnp.zeros(acc_sc.shape, F32)
    chains = [(c, b) for c in range(2) for b in range(nb)]
    qs = [q_ref[b * qb:(b + 1) * qb, c * QK_PAD:(c + 1) * QK_PAD] for c, b in chains]
    n = len(chains)
    ahead = ATT_LOOKAHEAD

    def scores(j, ci, rows):
        c, _ = chains[ci]
        k0 = pl.multiple_of(j * tk, tk)
        kc = k_ref[pl.ds(k0, rows), c * QK_PAD:(c + 1) * QK_PAD]
        return _dot_nt(kc, qs[ci])

    def accumulate(j, ci, s, m_old, kv_offset):
        c, b = chains[ci]
        rows = s.shape[0]
        k_base = ((j * tk) // POS_BLOCK) * POS_BLOCK
        shift = slope * (q_base - k_base).astype(F32)
        if kv_offset is not None:
            kv_i = lax.broadcasted_iota(jnp.int32, (rows, qb), 0) + kv_offset
            q_i = lax.broadcasted_iota(jnp.int32, (rows, qb), 1) + b * qb
            s = jnp.where(kv_i <= q_i, s, NEG_INF)
        m_new = jnp.maximum(m_old, jnp.max(s, axis=0, keepdims=True) - shift)
        p = jnp.exp(s - (m_new + shift)).astype(BF16)
        alpha = jnp.exp(m_old - m_new)
        cols = slice(b * qb, (b + 1) * qb)
        acc_sc[c, :, cols] = alpha * acc_sc[c, :, cols] + _dot(vt_ref[j, :, 0:rows], p)
        return m_new

    def run(tasks, ms, next_tile):
        ms = list(ms)
        early = {}
        for t, (j, ci, rows, kv_offset) in enumerate(tasks):
            nxt = t + ahead
            if nxt < len(tasks):
                early[nxt] = scores(tasks[nxt][0], tasks[nxt][1], tasks[nxt][2])
            elif next_tile is not None:
                s_sc[nxt - len(tasks)] = scores(next_tile, nxt - len(tasks), tk)
            s = s_sc[t, 0:rows] if t < ahead else early.pop(t)
            ms[ci] = accumulate(j, ci, s, ms[ci], kv_offset)
        return tuple(ms)

    for t in range(ahead):
        s_sc[t] = scores(0, t, tk)
    ms0 = tuple(jnp.full((1, qb), NEG_INF, F32) for _ in chains)
    first_diag = i * kv_per_q
    ms = lax.fori_loop(
        0, i,
        lambda g, ms: run([(g * kv_per_q + d, ci, tk, None) for d in range(kv_per_q) for ci in range(n)],
                          ms, (g + 1) * kv_per_q), ms0)

    tasks = []
    for d in range(kv_per_q):
        for ci, (c, b) in enumerate(chains):
            rows = min(tk, (b + 1) * qb - d * tk)
            if rows > 0:
                crosses = d * tk + rows - 1 > b * qb
                tasks.append((first_diag + d, ci, rows, d * tk if crosses else None))
    run(tasks, ms, None)

    d2 = 2 * DIFF_HEAD_DIM
    lam = (jnp.exp(jnp.sum(lq1_ref[...] * lk1_ref[...], axis=-1, keepdims=True))
           - jnp.exp(jnp.sum(lq2_ref[...] * lk2_ref[...], axis=-1, keepdims=True)) + LAM_INIT)
    a0 = acc_sc[0]
    a1 = acc_sc[1]
    o = a0[0:d2] / a0[d2:d2 + 1] - lam * (a1[0:d2] / a1[d2:d2 + 1])
    ms_o = jnp.mean(o * o, axis=0, keepdims=True)
    o = o * lax.rsqrt(ms_o + SUBLN_EPS) * sub_ref[...] * (1.0 - LAM_INIT)
    o_ref[...] = o.T.astype(o_ref.dtype)


def _diff_attention(qp, kp, vt, lq1, lk1, lq2, lk2, subln):
    S = qp.shape[0]
    tq = ATT_TQ
    d2 = 2 * DIFF_HEAD_DIM
    slopes = jnp.array(_alibi_slopes(), F32)
    vec = lambda z: z.reshape(1, -1).astype(F32)
    resident = pl.BlockSpec
    return pl.pallas_call(
        _attn_kernel,
        grid=(DIFF_HEADS, S // tq),
        in_specs=[
            pl.BlockSpec(memory_space=pltpu.SMEM),
            pl.BlockSpec((tq, 2 * QK_PAD), lambda h, i: (i, h)),
            resident((S, 2 * QK_PAD), lambda h, i: (0, h)),
            resident((None, S // ATT_TK, VT_ROWS, ATT_TK), lambda h, i: (h, 0, 0, 0)),
            _const_spec((1, DIFF_HEAD_DIM)), _const_spec((1, DIFF_HEAD_DIM)),
            _const_spec((1, DIFF_HEAD_DIM)), _const_spec((1, DIFF_HEAD_DIM)),
            _const_spec((d2, 1)),
        ],
        out_specs=pl.BlockSpec((tq, d2), lambda h, i: (i, h)),
        out_shape=jax.ShapeDtypeStruct((S, DIFF_DIM), BF16),
        scratch_shapes=[pltpu.VMEM((2, VT_ROWS, tq), F32),
                        pltpu.VMEM((ATT_LOOKAHEAD, ATT_TK, ATT_QB), F32)],
        compiler_params=pltpu.CompilerParams(
            dimension_semantics=("arbitrary", "arbitrary"), vmem_limit_bytes=VMEM_LIMIT),
        name="diff_attention",
    )(slopes, qp, kp, vt, vec(lq1), vec(lk1), vec(lq2), vec(lk2),
      subln.reshape(d2, 1).astype(F32))


def _gelu_tanh(x):
    return 0.5 * x * (1.0 + jnp.tanh(math.sqrt(2.0 / math.pi) * (x + 0.044715 * (x * x * x))))


def _ffn_kernel(x_ref, yr_ref, yo_ref, wo_ref, gpost_ref, gpre_ref, wup_ref, cw_ref, cb_ref,
                wdn_ref, gffn_ref, out_ref, tail_sc, hn_sc, acc_sc):
    tm = FFN_ROWS
    cw = FFN_CHUNK

    @pl.when(pl.program_id(0) == 0)
    def _():
        tail_sc[...] = jnp.zeros(tail_sc.shape, F32)

    def rms(z, g_row):
        ms = jnp.mean(z * z, axis=-1, keepdims=True)
        return z * lax.rsqrt(ms + NORM_EPS) * g_row

    half = RWKV_DIM
    mixed = _dot(yr_ref[...], wo_ref[0:half, :]) + _dot(yo_ref[...], wo_ref[half:2 * half, :])
    h = x_ref[...] + rms(mixed, gpost_ref[...])
    hn_sc[...] = rms(h, gpre_ref[...]).astype(BF16)

    sub8 = lax.broadcasted_iota(jnp.int32, (1, 8, cw), 1)

    def up(col0):
        return _dot(hn_sc[...], wup_ref[:, col0:col0 + cw])

    def conv(u, col0):
        tail = tail_sc[:, col0:col0 + cw]
        tail_sc[:, col0:col0 + cw] = u[tm - 8:tm]
        def shifted(k):
            rot = pltpu.roll(jnp.concatenate([tail, u], axis=0).reshape(tm // 8 + 1, 8, cw), k, axis=1)
            return jnp.where(sub8 < k, rot[:-1], rot[1:]).reshape(tm, cw)

        w = cw_ref[:, col0:col0 + cw]
        return (shifted(2) * w[0:1] + shifted(1) * w[1:2] + u * w[2:3]) + cb_ref[:, col0:col0 + cw]

    n_chunks = D_FF // cw
    nxt = (up(0), up(D_FF))
    for c in range(n_chunks):
        u_gate, u_val = nxt
        if c + 1 < n_chunks:
            nxt = (up((c + 1) * cw), up(D_FF + (c + 1) * cw))
        gate = conv(u_gate, c * cw)
        val = conv(u_val, D_FF + c * cw)
        act = (_gelu_tanh(gate) * val).astype(BF16)
        part = _dot(act, wdn_ref[c * cw:(c + 1) * cw, :])
        if c == 0:
            acc_sc[...] = part
        else:
            acc_sc[...] += part

    out_ref[...] = h + rms(acc_sc[...], gffn_ref[...])


def _out_and_ffn(x2, y_rwkv, y_attn, w_out, g_post, g_pre, w_up, conv_w, conv_b, w_down, g_ffn):
    S = x2.shape[0]
    tm = FFN_ROWS
    row = lambda z: z.reshape(1, -1).astype(F32)
    single = functools.partial(pl.BlockSpec, pipeline_mode=pl.Buffered(1))
    const1 = lambda shape: single(shape, lambda i: (0,) * len(shape))
    return pl.pallas_call(
        _ffn_kernel,
        grid=(S // tm,),
        in_specs=[
            pl.BlockSpec((tm, D_MODEL), lambda i: (i, 0)),
            pl.BlockSpec((tm, RWKV_DIM), lambda i: (i, 0)),
            pl.BlockSpec((tm, DIFF_DIM), lambda i: (i, 0)),
            const1((D_MODEL, D_MODEL)),
            _const_spec((1, D_MODEL)), _const_spec((1, D_MODEL)),
            const1((D_MODEL, 2 * D_FF)),
            _const_spec((CONV_WIDTH, 2 * D_FF)), _const_spec((1, 2 * D_FF)),
            const1((D_FF, D_MODEL)),
            _const_spec((1, D_MODEL)),
        ],
        out_specs=pl.BlockSpec((tm, D_MODEL), lambda i: (i, 0)),
        out_shape=jax.ShapeDtypeStruct((S, D_MODEL), F32),
        scratch_shapes=[
            pltpu.VMEM((8, 2 * D_FF), F32),
            pltpu.VMEM((tm, D_MODEL), BF16),
            pltpu.VMEM((tm, D_MODEL), F32),
        ],
        compiler_params=pltpu.CompilerParams(
            dimension_semantics=("arbitrary",), vmem_limit_bytes=VMEM_LIMIT),
        name="out_ffn",
    )(x2, y_rwkv, y_attn, w_out.astype(BF16), row(g_post), row(g_pre), w_up.astype(BF16),
      conv_w.astype(F32), row(conv_b), w_down.astype(BF16), row(g_ffn))


def kernel(x, ln_attn_pre, w_in, mu_shift, w_decay_up, w_decay0, w_iclr_up, w_iclr0, w_gate_up,
           k_k, k_a, r_k, ln_x_w, ln_x_b, lambda_q1, lambda_k1, lambda_q2, lambda_k2, diff_subln,
           w_out, ln_attn_post, ln_ffn_pre, w_up, conv_w, conv_b, w_down, ln_ffn_post):
    B, S, _ = x.shape
    assert B == 1 and w_in.shape[0] == 1, "single batch, depth 1"
    assert S % POS_BLOCK == 0
    x2 = x[0]
    xs, qp, kp, vt = _in_projection(x2, ln_attn_pre[0], mu_shift[0], w_in[0])
    y_rwkv = _rwkv_mix(xs, w_decay_up[0], w_decay0[0], w_iclr_up[0], w_iclr0[0],
                       w_gate_up[0], k_k[0], k_a[0], r_k[0], ln_x_w[0], ln_x_b[0])
    y_attn = _diff_attention(qp, kp, vt, lambda_q1[0], lambda_k1[0], lambda_q2[0], lambda_k2[0],
                             diff_subln[0])
    out = _out_and_ffn(x2, y_rwkv, y_attn, w_out[0], ln_attn_post[0], ln_ffn_pre[0], w_up[0],
                       conv_w[0], conv_b[0], w_down[0], ln_ffn_post[0])
    return out[None]
```

```python
import functools
import math

import jax
import jax.numpy as jnp
import numpy as np
from jax import lax
from jax.experimental import pallas as pl
from jax.experimental.pallas import tpu as pltpu

F32 = jnp.float32
BF16 = jnp.bfloat16

D_MODEL = 1024
RWKV_HEADS = 8
RWKV_HEAD_DIM = 64
RWKV_DIM = RWKV_HEADS * RWKV_HEAD_DIM
LORA_W = 64
LORA_A = 64
LORA_G = 128
DIFF_HEADS = 4
DIFF_HEAD_DIM = 64
DIFF_DIM = DIFF_HEADS * 2 * DIFF_HEAD_DIM
RWKV_COLS = 3 * RWKV_DIM + LORA_W + LORA_A + LORA_G
D_FF = 2816
CONV_WIDTH = 3
DECAY_SCALE = math.exp(-0.5)
RWKV_LN_EPS = 64e-5
NORM_EPS = 1e-6
SUBLN_EPS = 1e-5
NEG_INF = -1e30
LAM_INIT = 0.8 - 0.6 * math.exp(-0.3 * 0)

LANES = 128
VMEM_LIMIT = 56 * 1024 * 1024

ATT_TQ = 1024
ATT_TK = 512
ATT_QB = 256
ATT_LOOKAHEAD = 3
POS_BLOCK = 1024
POS_SPLIT = 256
N_MAPS = 2 * DIFF_HEADS
QK_PAD = 128
VT_ROWS = 144

CHUNK = 64
RWKV_ROWS = 256
N_PAIRS = RWKV_HEADS // 2

PROJ_ROWS = 512
FFN_ROWS = 256
FFN_CHUNK = 256


def _dot(a, b):
    return jnp.dot(a, b, preferred_element_type=F32)


def _dot_nt(a, b):
    return lax.dot_general(a, b, (((1,), (1,)), ((), ())), preferred_element_type=F32)


def _dot_tn(a, b):
    return lax.dot_general(a, b, (((0,), (0,)), ((), ())), preferred_element_type=F32)


def _const_spec(shape):
    nd = len(shape)
    return pl.BlockSpec(shape, lambda *_: (0,) * nd)


def _inproj_kernel(x_ref, g_ref, mu_ref, w_ref, wvt_ref, qpat_ref, kpat_ref, vtpat_ref,
                   rw_ref, q_ref, k_ref, vt_ref, prev_sc):
    tm = PROJ_ROWS

    @pl.when(pl.program_id(0) == 0)
    def _():
        prev_sc[...] = jnp.zeros(prev_sc.shape, F32)

    x = x_ref[...]
    ms = jnp.mean(x * x, axis=-1, keepdims=True)
    xn = (x * lax.rsqrt(ms + NORM_EPS) * g_ref[...]).astype(BF16)
    c0 = RWKV_COLS
    c1 = c0 + DIFF_DIM
    c2 = c1 + DIFF_DIM
    p = _dot(xn, w_ref[:, 0:c0])
    row = lax.broadcasted_iota(jnp.int32, (8, RWKV_COLS), 0)
    shifted = pltpu.roll(p, 1, axis=0)
    first = jnp.where(row == 0, prev_sc[...], shifted[0:8])
    shifted = jnp.concatenate([first, shifted[8:]], axis=0)
    prev_sc[...] = jnp.broadcast_to(p[tm - 1:tm, :], prev_sc.shape)
    rw_ref[...] = p + (shifted - p) * mu_ref[...]

    low_half = lax.broadcasted_iota(jnp.int32, (tm, LANES), 1) < DIFF_HEAD_DIM

    def spread(z, pat_ref, out_ref):
        for j in range(DIFF_DIM // LANES):
            two_maps = z[:, j * LANES:(j + 1) * LANES]
            for odd in range(2):
                m = 2 * j + odd
                piece = pltpu.roll(two_maps, DIFF_HEAD_DIM, axis=1) if odd else two_maps
                slot = slice(m * QK_PAD, (m + 1) * QK_PAD)
                out_ref[:, slot] = (jnp.where(low_half, piece, 0.0) + pat_ref[:, slot]).astype(BF16)

    spread(_dot(xn, w_ref[:, c0:c1]), qpat_ref, q_ref)
    spread(_dot(xn, w_ref[:, c1:c2]), kpat_ref, k_ref)
    vt =(_dot_nt(wvt_ref[...], xn) + vtpat_ref[...]).astype(BF16)
    for h in range(DIFF_HEADS):
        vt_ref[h, 0] = vt[h * VT_ROWS:(h + 1) * VT_ROWS]


def _alibi_slopes():
    return [2.0 ** (-8.0 * (i + 1) / DIFF_HEADS) for i in range(DIFF_HEADS)]


def _alibi_patterns():
    pos = np.arange(POS_BLOCK)
    lo = (pos % POS_SPLIT).astype(np.float32)
    hi = (pos - pos % POS_SPLIT).astype(np.float32)
    qpat = np.zeros((POS_BLOCK, N_MAPS, QK_PAD), np.float32)
    kpat = np.zeros((POS_BLOCK, N_MAPS, QK_PAD), np.float32)
    d = DIFF_HEAD_DIM
    for m in range(N_MAPS):
        slope = _alibi_slopes()[m // 2]
        qpat[:, m, d + 0] = -slope * lo
        qpat[:, m, d + 1] = -slope * hi
        qpat[:, m, d + 2:d + 4] = 1.0
        kpat[:, m, d + 0:d + 2] = 1.0
        kpat[:, m, d + 2] = slope * lo
        kpat[:, m, d + 3] = slope * hi
    vtpat = np.zeros((DIFF_HEADS, VT_ROWS, PROJ_ROWS), np.float32)
    vtpat[:, 2 * d, :] = 1.0
    return (jnp.asarray(qpat.reshape(POS_BLOCK, N_MAPS * QK_PAD), BF16),
            jnp.asarray(kpat.reshape(POS_BLOCK, N_MAPS * QK_PAD), BF16),
            jnp.asarray(vtpat.reshape(DIFF_HEADS * VT_ROWS, PROJ_ROWS), BF16))


def _pack_w_in(w_in):
    d = DIFF_HEAD_DIM
    w_rw = w_in[:, :RWKV_COLS]
    wq = w_in[:, RWKV_COLS:RWKV_COLS + DIFF_DIM] * (d ** -0.5)
    wk = w_in[:, RWKV_COLS + DIFF_DIM:RWKV_COLS + 2 * DIFF_DIM]
    wv = w_in[:, RWKV_COLS + 2 * DIFF_DIM:].reshape(D_MODEL, DIFF_HEADS, 2 * d)
    wvt = jnp.pad(wv.transpose(1, 2, 0), ((0, 0), (0, VT_ROWS - 2 * d), (0, 0)))
    return (jnp.concatenate([w_rw, wq, wk], axis=1).astype(BF16),
            wvt.reshape(DIFF_HEADS * VT_ROWS, D_MODEL).astype(BF16))


def _in_projection(x2, g, mu, w_in):
    S = x2.shape[0]
    tm = PROJ_ROWS
    assert tm == ATT_TK, "the transposed value blocks are consumed one per attention kv step"
    w, wvt = _pack_w_in(w_in)
    qpat, kpat, vtpat = _alibi_patterns()
    nq = N_MAPS * QK_PAD
    rep = POS_BLOCK // tm
    single = functools.partial(pl.BlockSpec, pipeline_mode=pl.Buffered(1))
    return pl.pallas_call(
        _inproj_kernel,
        grid=(S // tm,),
        in_specs=[
            pl.BlockSpec((tm, D_MODEL), lambda i: (i, 0)),
            _const_spec((1, D_MODEL)),
            _const_spec((1, RWKV_COLS)),
            single(w.shape, lambda i: (0, 0)),
            single(wvt.shape, lambda i: (0, 0)),
            pl.BlockSpec((tm, nq), lambda i: (i % rep, 0)),
            pl.BlockSpec((tm, nq), lambda i: (i % rep, 0)),
            _const_spec(vtpat.shape),
        ],
        out_specs=[
            pl.BlockSpec((tm, RWKV_COLS), lambda i: (i, 0)),
            pl.BlockSpec((tm, nq), lambda i: (i, 0)),
            pl.BlockSpec((tm, nq), lambda i: (i, 0)),
            pl.BlockSpec((DIFF_HEADS, 1, VT_ROWS, tm), lambda i: (0, i, 0, 0)),
        ],
        out_shape=[
            jax.ShapeDtypeStruct((S, RWKV_COLS), F32),
            jax.ShapeDtypeStruct((S, nq), BF16),
            jax.ShapeDtypeStruct((S, nq), BF16),
            jax.ShapeDtypeStruct((DIFF_HEADS, S // tm, VT_ROWS, tm), BF16),
        ],
        scratch_shapes=[pltpu.VMEM((8, RWKV_COLS), F32)],
        compiler_params=pltpu.CompilerParams(
            dimension_semantics=("arbitrary",), vmem_limit_bytes=VMEM_LIMIT),
        name="in_projection",
    )(x2, g.reshape(1, D_MODEL), mu.reshape(1, RWKV_COLS).astype(F32), w, wvt, qpat, kpat, vtpat)


def _rwkv_kernel(xs_ref, wwa_ref, wd0_ref, wa0_ref, wg_ref, kk_ref, ka_ref, rk_ref,
                 lnw_ref, lnb_ref, hsum_ref, cum_ref, y_ref, state_sc, ych_sc):
    T = RWKV_ROWS
    C = CHUNK
    D = RWKV_DIM

    @pl.when(pl.program_id(0) == 0)
    def _():
        state_sc[...] = jnp.zeros(state_sc.shape, F32)

    r = xs_ref[:, 0:D]
    k = xs_ref[:, D:2 * D]
    v = xs_ref[:, 2 * D:3 * D]
    wa = xs_ref[:, 3 * D:3 * D + LANES]
    g_lo = xs_ref[:, 3 * D + LANES:3 * D + 2 * LANES]

    lane = lax.broadcasted_iota(jnp.int32, (T, LANES), 1)
    wa_act = jnp.where(lane < LORA_W, jnp.tanh(wa), wa).astype(BF16)
    pre = _dot(wa_act, wwa_ref[...])
    lw = -DECAY_SCALE * jax.nn.sigmoid(wd0_ref[...] + pre[:, 0:D])
    a = jax.nn.sigmoid(wa0_ref[...] + pre[:, D:2 * D])
    g = _dot(jax.nn.sigmoid(g_lo).astype(BF16), wg_ref[...])

    def head_sum(z):
        zb = z.astype(BF16)
        w = hsum_ref.shape[0]
        return jnp.concatenate([_dot(zb[:, o:o + w], hsum_ref[...]) for o in range(0, D, w)], axis=1)

    kk = k * kk_ref[...]
    n2 = head_sum(kk * kk)
    kk = kk * jnp.minimum(lax.rsqrt(n2), 1e12)
    k2 = k * (1.0 + (a - 1.0) * ka_ref[...])
    bonus = head_sum(r * k2 * rk_ref[...])

    lw_hi = lw.astype(BF16)
    lw_lo = (lw - lw_hi.astype(F32)).astype(BF16)
    cum = _dot(cum_ref[...], lw_hi) + _dot(cum_ref[...], lw_lo)
    L = cum[0:T]
    LC = cum[T:2 * T]
    e_fwd = jnp.exp(L)
    e_prev = jnp.exp(L - lw)
    e_inv = jnp.exp(-L)
    e_end = jnp.exp(LC - L)
    p_end = jnp.exp(LC)
    kka = kk * a
    r_t = (r * e_fwd).astype(BF16)
    al_t = (-kk * e_prev)
    k_t = (k2 * e_inv).astype(BF16)
    be_t = (kka * e_inv).astype(BF16)
    k_e = (k2 * e_end).astype(BF16)
    be_e = (kka * e_end).astype(BF16)
    v_b = v.astype(BF16)

    rr = lax.broadcasted_iota(jnp.int32, (2 * C, LANES), 0)
    ll = lax.broadcasted_iota(jnp.int32, (2 * C, LANES), 1)
    bd_mask = (rr < C) == (ll < C)
    tri_r = lax.broadcasted_iota(jnp.int32, (C, LANES), 0)
    tri_l = lax.broadcasted_iota(jnp.int32, (C, LANES), 1) & (C - 1)
    strict = tri_l < tri_r
    incl = tri_l <= tri_r

    def bd(z):
        zb = z.astype(BF16)
        return jnp.where(bd_mask, jnp.concatenate([zb, zb], axis=0), jnp.zeros((), BF16))

    n_chunks = T // C
    pairs = range(N_PAIRS)
    blocks = [(c, pr) for c in range(n_chunks) for pr in pairs]
    blk = {(c, pr): (slice(c * C, (c + 1) * C), slice(pr * LANES, (pr + 1) * LANES))
           for c, pr in blocks}

    gram = {}
    for key in blocks:
        sl = blk[key]
        xs_c = jnp.concatenate([al_t[sl].astype(BF16), r_t[sl]], axis=0)
        y_bd = jnp.concatenate([bd(be_t[sl]), bd(k_t[sl])], axis=0)
        gram[key] = _dot_nt(xs_c, y_bd)
    a_pow = {key: jnp.where(strict, gram[key][0:C, 0:LANES], 0.0) for key in blocks}
    a_r = {key: jnp.concatenate([jnp.where(incl, gram[key][C:2 * C, 0:LANES], 0.0),
                                 jnp.where(incl, gram[key][C:2 * C, LANES:2 * LANES], 0.0)],
                                axis=1).astype(BF16) for key in blocks}
    v_bd = {key: bd(v_b[blk[key]]) for key in blocks}

    x_u = {key: al_t[blk[key]] for key in blocks}
    x_0 = {key: _dot(jnp.where(strict, gram[key][0:C, LANES:2 * LANES], 0.0).astype(BF16), v_bd[key])
           for key in blocks}
    n_steps = int(math.log2(C))
    for step in range(n_steps):
        a_pow_b = {key: a_pow[key].astype(BF16) for key in blocks}
        for key in blocks:
            rhs = jnp.concatenate([bd(x_u[key]), bd(x_0[key])], axis=1)
            upd = _dot(a_pow_b[key], rhs)
            x_u[key] = x_u[key] + upd[:, 0:LANES]
            x_0[key] = x_0[key] + upd[:, LANES:2 * LANES]
        if step + 1 < n_steps:
            a_pow = {key: _dot(a_pow_b[key], bd(a_pow[key])) for key in blocks}

    w_mat = {}
    n_mat = {}
    for key in blocks:
        sl = blk[key]
        w_mat[key] = jnp.where(bd_mask, _dot_tn(x_u[key].astype(BF16), be_e[sl]), 0.0).astype(BF16)
        vu0 = jnp.concatenate([v_b[sl], x_0[key].astype(BF16)], axis=0)
        kb = jnp.concatenate([k_e[sl], be_e[sl]], axis=0)
        n_mat[key] = jnp.where(bd_mask, _dot_tn(vu0, kb), 0.0)

    def emit_outputs(c, u, y_s):
        for pr in pairs:
            ych_sc[blk[c, pr]] = y_s[pr] + _dot(
                a_r[c, pr], jnp.concatenate([bd(u[pr]), v_bd[c, pr]], axis=0))

    st = {pr: state_sc[pr] for pr in pairs}
    pending = None
    for c in range(n_chunks):
        st_b = {pr: st[pr].astype(BF16) for pr in pairs}
        st = {pr: st[pr] * p_end[c * C:c * C + 1, blk[c, pr][1]] + _dot(st_b[pr], w_mat[c, pr])
              + n_mat[c, pr] for pr in pairs}
        if pending is not None:
            emit_outputs(*pending)
        u = {pr: _dot_nt(x_u[c, pr].astype(BF16), st_b[pr]) + x_0[c, pr] for pr in pairs}
        y_s = {pr: _dot_nt(r_t[blk[c, pr]], st_b[pr]) for pr in pairs}
        pending = (c, u, y_s)
    emit_outputs(*pending)
    for pr in pairs:
        state_sc[pr] = st[pr]

    y = ych_sc[...] + bonus * v
    inv_n = 1.0 / RWKV_HEAD_DIM
    mean = head_sum(y) * inv_n
    yc = y - mean
    var = head_sum(yc * yc) * inv_n
    yn = yc * lax.rsqrt(var + RWKV_LN_EPS)
    y_ref[...] = ((yn * lnw_ref[...] + lnb_ref[...]) * g).astype(y_ref.dtype)


def _rwkv_mix(xs, w_decay_up, w_decay0, w_iclr_up, w_iclr0, w_gate_up, k_k, k_a, r_k,
              ln_x_w, ln_x_b):
    S = xs.shape[0]
    T = RWKV_ROWS
    D = RWKV_DIM
    wwa = jnp.zeros((LANES, 2 * D), F32)
    wwa = wwa.at[0:LORA_W, 0:D].set(w_decay_up).at[LORA_W:LORA_W + LORA_A, D:2 * D].set(w_iclr_up)
    hid = np.arange(2 * LANES) // RWKV_HEAD_DIM
    hsum = jnp.asarray(hid[:, None] == hid[None, :], BF16)
    t = np.arange(T)
    same = (t[:, None] // CHUNK) == (t[None, :] // CHUNK)
    cum = jnp.asarray(np.concatenate([same & (t[None, :] <= t[:, None]), same], axis=0), BF16)
    row = lambda z: z.reshape(1, -1).astype(F32)
    args = (xs, wwa.astype(BF16), row(w_decay0), row(w_iclr0), w_gate_up.astype(BF16),
            row(k_k), row(k_a), row(r_k), row(ln_x_w), row(ln_x_b), hsum, cum)
    in_specs = [pl.BlockSpec((T, RWKV_COLS), lambda i: (i, 0))]
    in_specs += [_const_spec(z.shape) for z in args[1:]]
    return pl.pallas_call(
        _rwkv_kernel,
        grid=(S // T,),
        in_specs=in_specs,
        out_specs=pl.BlockSpec((T, D), lambda i: (i, 0)),
        out_shape=jax.ShapeDtypeStruct((S, D), BF16),
        scratch_shapes=[
            pltpu.VMEM((N_PAIRS, LANES, LANES), F32),
            pltpu.VMEM((T, D), F32),
        ],
        compiler_params=pltpu.CompilerParams(
            dimension_semantics=("arbitrary",), vmem_limit_bytes=VMEM_LIMIT),
        name="rwkv7_mix",
    )(*args)


def _attn_kernel(slopes_ref, q_ref, k_ref, vt_ref, lq1_ref, lk1_ref, lq2_ref, lk2_ref, sub_ref,
                 o_ref, acc_sc, s_sc):
    tq = ATT_TQ
    tk = ATT_TK
    qb = ATT_QB
    nb = tq // qb
    kv_per_q = tq // tk
    h = pl.program_id(0)
    i = pl.program_id(1)
    slope = slopes_ref[h]
    q_base = ((i * tq) // POS_BLOCK) * POS_BLOCK

    acc_sc[...] = jnp.zeros(acc_sc.shape, F32)
    chains = [(c, b) for c in range(2) for b in range(nb)]
    qs = [q_ref[b * qb:(b + 1) * qb, c * QK_PAD:(c + 1) * QK_PAD] for c, b in chains]
    n = len(chains)
    ahead = ATT_LOOKAHEAD

    def scores(j, ci, rows):
        c, _ = chains[ci]
        k0 = pl.multiple_of(j * tk, tk)
        kc = k_ref[pl.ds(k0, rows), c * QK_PAD:(c + 1) * QK_PAD]
        return _dot_nt(kc, qs[ci])

    def accumulate(j, ci, s, m_old, kv_offset):
        c, b = chains[ci]
        rows = s.shape[0]
        k_base = ((j * tk) // POS_BLOCK) * POS_BLOCK
        shift = slope * (q_base - k_base).astype(F32)
        if kv_offset is not None:
            kv_i = lax.broadcasted_iota(jnp.int32, (rows, qb), 0) + kv_offset
            q_i = lax.broadcasted_iota(jnp.int32, (rows, qb), 1) + b * qb
            s = jnp.where(kv_i <= q_i, s, NEG_INF)
        m_new = jnp.maximum(m_old, jnp.max(s, axis=0, keepdims=True) - shift)
        p = jnp.exp(s - (m_new + shift)).astype(BF16)
        alpha = jnp.exp(m_old - m_new)
        cols = slice(b * qb, (b + 1) * qb)
        acc_sc[c, :, cols] = alpha * acc_sc[c, :, cols] + _dot(vt_ref[j, :, 0:rows], p)
        return m_new

    def run(tasks, ms, next_tile):
        ms = list(ms)
        early = {}
        for t, (j, ci, rows, kv_offset) in enumerate(tasks):
            nxt = t + ahead
            if nxt < len(tasks):
                early[nxt] = scores(tasks[nxt][0], tasks[nxt][1], tasks[nxt][2])
            elif next_tile is not None:
                s_sc[nxt - len(tasks)] = scores(next_tile, nxt - len(tasks), tk)
            s = s_sc[t, 0:rows] if t < ahead else early.pop(t)
            ms[ci] = accumulate(j, ci, s, ms[ci], kv_offset)
        return tuple(ms)

    for t in range(ahead):
        s_sc[t] = scores(0, t, tk)
    ms0 = tuple(jnp.full((1, qb), NEG_INF, F32) for _ in chains)
    first_diag = i * kv_per_q
    ms = lax.fori_loop(
        0, i,
        lambda g, ms: run([(g * kv_per_q + d, ci, tk, None) for d in range(kv_per_q) for ci in range(n)],
                          ms, (g + 1) * kv_per_q), ms0)

    tasks = []
    for d in range(kv_per_q):
        for ci, (c, b) in enumerate(chains):
            rows = min(tk, (b + 1) * qb - d * tk)
            if rows > 0:
                crosses = d * tk + rows - 1 > b * qb
                tasks.append((first_diag + d, ci, rows, d * tk if crosses else None))
    run(tasks, ms, None)

    d2 = 2 * DIFF_HEAD_DIM
    lam = (jnp.exp(jnp.sum(lq1_ref[...] * lk1_ref[...], axis=-1, keepdims=True))
           - jnp.exp(jnp.sum(lq2_ref[...] * lk2_ref[...], axis=-1, keepdims=True)) + LAM_INIT)
    a0 = acc_sc[0]
    a1 = acc_sc[1]
    o = a0[0:d2] / a0[d2:d2 + 1] - lam * (a1[0:d2] / a1[d2:d2 + 1])
    ms_o = jnp.mean(o * o, axis=0, keepdims=True)
    o = o * lax.rsqrt(ms_o + SUBLN_EPS) * sub_ref[...] * (1.0 - LAM_INIT)
    o_ref[...] = o.T.astype(o_ref.dtype)


def _diff_attention(qp, kp, vt, lq1, lk1, lq2, lk2, subln):
    S = qp.shape[0]
    tq = ATT_TQ
    d2 = 2 * DIFF_HEAD_DIM
    slopes = jnp.array(_alibi_slopes(), F32)
    vec = lambda z: z.reshape(1, -1).astype(F32)
    resident = pl.BlockSpec
    return pl.pallas_call(
        _attn_kernel,
        grid=(DIFF_HEADS, S // tq),
        in_specs=[
            pl.BlockSpec(memory_space=pltpu.SMEM),
            pl.BlockSpec((tq, 2 * QK_PAD), lambda h, i: (i, h)),
            resident((S, 2 * QK_PAD), lambda h, i: (0, h)),
            resident((None, S // ATT_TK, VT_ROWS, ATT_TK), lambda h, i: (h, 0, 0, 0)),
            _const_spec((1, DIFF_HEAD_DIM)), _const_spec((1, DIFF_HEAD_DIM)),
            _const_spec((1, DIFF_HEAD_DIM)), _const_spec((1, DIFF_HEAD_DIM)),
            _const_spec((d2, 1)),
        ],
        out_specs=pl.BlockSpec((tq, d2), lambda h, i: (i, h)),
        out_shape=jax.ShapeDtypeStruct((S, DIFF_DIM), BF16),
        scratch_shapes=[pltpu.VMEM((2, VT_ROWS, tq), F32),
                        pltpu.VMEM((ATT_LOOKAHEAD, ATT_TK, ATT_QB), F32)],
        compiler_params=pltpu.CompilerParams(
            dimension_semantics=("arbitrary", "arbitrary"), vmem_limit_bytes=VMEM_LIMIT),
        name="diff_attention",
    )(slopes, qp, kp, vt, vec(lq1), vec(lk1), vec(lq2), vec(lk2),
      subln.reshape(d2, 1).astype(F32))


def _gelu_tanh(x):
    return 0.5 * x * (1.0 + jnp.tanh(math.sqrt(2.0 / math.pi) * (x + 0.044715 * (x * x * x))))


def _ffn_kernel(x_ref, yr_ref, yo_ref, wo_ref, gpost_ref, gpre_ref, wup_ref, cw_ref, cb_ref,
                wdn_ref, gffn_ref, out_ref, tail_sc, hn_sc, acc_sc):
    tm = FFN_ROWS
    cw = FFN_CHUNK

    @pl.when(pl.program_id(0) == 0)
    def _():
        tail_sc[...] = jnp.zeros(tail_sc.shape, F32)

    def rms(z, g_row):
        ms = jnp.mean(z * z, axis=-1, keepdims=True)
        return z * lax.rsqrt(ms + NORM_EPS) * g_row

    half = RWKV_DIM
    mixed = _dot(yr_ref[...], wo_ref[0:half, :]) + _dot(yo_ref[...], wo_ref[half:2 * half, :])
    h = x_ref[...] + rms(mixed, gpost_ref[...])
    hn_sc[...] = rms(h, gpre_ref[...]).astype(BF16)

    sub8 = lax.broadcasted_iota(jnp.int32, (1, 8, cw), 1)

    def up(col0):
        return _dot(hn_sc[...], wup_ref[:, col0:col0 + cw])

    def conv(u, col0):
        tail = tail_sc[:, col0:col0 + cw]
        tail_sc[:, col0:col0 + cw] = u[tm - 8:tm]
        def shifted(k):
            rot = pltpu.roll(jnp.concatenate([tail, u], axis=0).reshape(tm // 8 + 1, 8, cw), k, axis=1)
            return jnp.where(sub8 < k, rot[:-1], rot[1:]).reshape(tm, cw)

        w = cw_ref[:, col0:col0 + cw]
        return (shifted(2) * w[0:1] + shifted(1) * w[1:2] + u * w[2:3]) + cb_ref[:, col0:col0 + cw]

    n_chunks = D_FF // cw
    nxt = (up(0), up(D_FF))
    for c in range(n_chunks):
        u_gate, u_val = nxt
        if c + 1 < n_chunks:
            nxt = (up((c + 1) * cw), up(D_FF + (c + 1) * cw))
        gate = conv(u_gate, c * cw)
        val = conv(u_val, D_FF + c * cw)
        act = (_gelu_tanh(gate) * val).astype(BF16)
        part = _dot(act, wdn_ref[c * cw:(c + 1) * cw, :])
        if c == 0:
            acc_sc[...] = part
        else:
            acc_sc[...] += part

    out_ref[...] = h + rms(acc_sc[...], gffn_ref[...])


def _out_and_ffn(x2, y_rwkv, y_attn, w_out, g_post, g_pre, w_up, conv_w, conv_b, w_down, g_ffn):
    S = x2.shape[0]
    tm = FFN_ROWS
    row = lambda z: z.reshape(1, -1).astype(F32)
    single = functools.partial(pl.BlockSpec, pipeline_mode=pl.Buffered(1))
    const1 = lambda shape: single(shape, lambda i: (0,) * len(shape))
    return pl.pallas_call(
        _ffn_kernel,
        grid=(S // tm,),
        in_specs=[
            pl.BlockSpec((tm, D_MODEL), lambda i: (i, 0)),
            pl.BlockSpec((tm, RWKV_DIM), lambda i: (i, 0)),
            pl.BlockSpec((tm, DIFF_DIM), lambda i: (i, 0)),
            const1((D_MODEL, D_MODEL)),
            _const_spec((1, D_MODEL)), _const_spec((1, D_MODEL)),
            const1((D_MODEL, 2 * D_FF)),
            _const_spec((CONV_WIDTH, 2 * D_FF)), _const_spec((1, 2 * D_FF)),
            const1((D_FF, D_MODEL)),
            _const_spec((1, D_MODEL)),
        ],
        out_specs=pl.BlockSpec((tm, D_MODEL), lambda i: (i, 0)),
        out_shape=jax.ShapeDtypeStruct((S, D_MODEL), F32),
        scratch_shapes=[
            pltpu.VMEM((8, 2 * D_FF), F32),
            pltpu.VMEM((tm, D_MODEL), BF16),
            pltpu.VMEM((tm, D_MODEL), F32),
        ],
        compiler_params=pltpu.CompilerParams(
            dimension_semantics=("arbitrary",), vmem_limit_bytes=VMEM_LIMIT),
        name="out_ffn",
    )(x2, y_rwkv, y_attn, w_out.astype(BF16), row(g_post), row(g_pre), w_up.astype(BF16),
      conv_w.astype(F32), row(conv_b), w_down.astype(BF16), row(g_ffn))


def kernel(x, ln_attn_pre, w_in, mu_shift, w_decay_up, w_decay0, w_iclr_up, w_iclr0, w_gate_up,
           k_k, k_a, r_k, ln_x_w, ln_x_b, lambda_q1, lambda_k1, lambda_q2, lambda_k2, diff_subln,
           w_out, ln_attn_post, ln_ffn_pre, w_up, conv_w, conv_b, w_down, ln_ffn_post):
    B, S, _ = x.shape
    assert B == 1 and w_in.shape[0] == 1, "single batch, depth 1"
    assert S % POS_BLOCK == 0
    x2 = x[0]
    xs, qp, kp, vt = _in_projection(x2, ln_attn_pre[0], mu_shift[0], w_in[0])
    y_rwkv = _rwkv_mix(xs, w_decay_up[0], w_decay0[0], w_iclr_up[0], w_iclr0[0],
                       w_gate_up[0], k_k[0], k_a[0], r_k[0], ln_x_w[0], ln_x_b[0])
    y_attn = _diff_attention(qp, kp, vt, lambda_q1[0], lambda_k1[0], lambda_q2[0], lambda_k2[0],
                             diff_subln[0])
    out = _out_and_ffn(x2, y_rwkv, y_attn, w_out[0], ln_attn_post[0], ln_ffn_pre[0], w_up[0],
                       conv_w[0], conv_b[0], w_down[0], ln_ffn_post[0])
    return out[None]
```
